```python
import math
import jax, jax.numpy as jnp
from jax import lax
import numpy as np

D_MODEL = 1024
BATCH = 32
SEQ = 2048
DEPTH = 1

SSD_HEADS = 16
SSD_HEAD_DIM = 64
SSD_WIDTH = SSD_HEADS * SSD_HEAD_DIM
SSD_GROUPS = 2
SSD_STATE = 128
CONV_WIDTH = 4
CHUNK = 128
CONV_CH = SSD_WIDTH + 2 * SSD_GROUPS * SSD_STATE
ATT_HEADS = 16
ATT_HEAD_DIM = 64
ATT_WIDTH = ATT_HEADS * ATT_HEAD_DIM
Q_BLOCK = 128
MIX_WIDTH = SSD_WIDTH + ATT_WIDTH
IN_SIZES = (SSD_WIDTH, CONV_CH, SSD_HEADS, ATT_WIDTH, ATT_WIDTH, ATT_WIDTH, ATT_HEADS)
IN_WIDTH = sum(IN_SIZES)
D_FF = 4 * D_MODEL
EPS = 1e-5

kernel_name = "hymba_ssd_fox_sqrelu_block"


def rmsnorm(x, w):
    xf = x.astype(jnp.float32)
    y = xf * lax.rsqrt(jnp.mean(xf * xf, axis=-1, keepdims=True) + EPS)
    return (y * w.astype(jnp.float32)).astype(x.dtype)


def causal_dwconv(u, w, b):
    out = lax.conv_general_dilated(
        u, w[:, None, :].astype(u.dtype), window_strides=(1,),
        padding=[(CONV_WIDTH - 1, 0)], dimension_numbers=("NWC", "WIO", "NWC"),
        feature_group_count=u.shape[-1])
    return out + b


def segsum(a):
    cum = jnp.cumsum(a, axis=-1)
    diff = cum[..., :, None] - cum[..., None, :]
    l = a.shape[-1]
    mask = jnp.tril(jnp.ones((l, l), dtype=bool))
    return jnp.where(mask, diff, -jnp.inf)


def ssd_chunked(xh, a, bmat, cmat):
    bsz, T = xh.shape[:2]
    nc = T // CHUNK
    r = SSD_HEADS // SSD_GROUPS
    f32 = jnp.float32
    x = xh.astype(f32).reshape(bsz, nc, CHUNK, SSD_GROUPS, r, SSD_HEAD_DIM)
    a = a.astype(f32).reshape(bsz, nc, CHUNK, SSD_GROUPS, r).transpose(0, 3, 4, 1, 2)
    B = bmat.astype(f32).reshape(bsz, nc, CHUNK, SSD_GROUPS, SSD_STATE)
    C = cmat.astype(f32).reshape(bsz, nc, CHUNK, SSD_GROUPS, SSD_STATE)
    a_cum = jnp.cumsum(a, axis=-1)
    ldec = jnp.exp(segsum(a))
    cb = jnp.einsum("bclgn,bcsgn->bcgls", C, B)
    y_diag = jnp.einsum("bcgls,bgrcls,bcsgrp->bclgrp", cb, ldec, x)
    decay_states = jnp.exp(a_cum[..., -1:] - a_cum)
    states = jnp.einsum("bclgn,bgrcl,bclgrp->bcgrpn", B, decay_states, x)
    chunk_decay = jnp.exp(a_cum[..., -1])

    def step(h, inp):
        s_c, d_c = inp
        return h * d_c[..., None, None] + s_c, h

    h0 = jnp.zeros((bsz, SSD_GROUPS, r, SSD_HEAD_DIM, SSD_STATE), f32)
    _, prev = lax.scan(step, h0, (jnp.moveaxis(states, 1, 0), jnp.moveaxis(chunk_decay, -1, 0)))
    y_off = jnp.einsum("bclgn,cbgrpn,bgrcl->bclgrp", C, prev, jnp.exp(a_cum))
    return (y_diag + y_off).reshape(bsz, T, SSD_HEADS, SSD_HEAD_DIM)


def forgetting_attention(q, k, v, log_f):
    T = q.shape[2]
    scale = 1.0 / math.sqrt(ATT_HEAD_DIM)
    c = jnp.cumsum(log_f, axis=-1)
    outs = []
    for i in range(T // Q_BLOCK):
        qs, qe = i * Q_BLOCK, (i + 1) * Q_BLOCK
        s = jnp.einsum("bhqd,bhkd->bhqk", q[:, :, qs:qe], k[:, :, :qe]).astype(jnp.float32) * scale
        s = s + (c[:, :, qs:qe, None] - c[:, :, None, :qe])
        mask = (qs + jnp.arange(Q_BLOCK))[:, None] >= jnp.arange(qe)[None, :]
        p = jax.nn.softmax(jnp.where(mask, s, -jnp.inf), axis=-1)
        outs.append(jnp.einsum("bhqk,bhkd->bhqd", p.astype(v.dtype), v[:, :, :qe]))
    return jnp.concatenate(outs, axis=2)


def hybrid_mixer(h, w_in, conv_w, conv_b, dt_bias, a_log, d_skip, ssd_norm_w, f_bias, w_out):
    bsz, T, _ = h.shape
    proj = jnp.einsum("btd,de->bte", h, w_in)
    idx = np.cumsum(IN_SIZES)[:-1].tolist()
    z, xbc, dt_raw, q, k, v, f_raw = jnp.split(proj, idx, axis=-1)
    xbc = jax.nn.silu(causal_dwconv(xbc, conv_w, conv_b))
    xs, bm, cm = jnp.split(xbc, [SSD_WIDTH, SSD_WIDTH + SSD_GROUPS * SSD_STATE], axis=-1)
    xs = xs.reshape(bsz, T, SSD_HEADS, SSD_HEAD_DIM)
    bm = bm.reshape(bsz, T, SSD_GROUPS, SSD_STATE)
    cm = cm.reshape(bsz, T, SSD_GROUPS, SSD_STATE)
    dt = jax.nn.softplus(dt_raw.astype(jnp.float32) + dt_bias.astype(jnp.float32))
    A = -jnp.exp(a_log.astype(jnp.float32))
    y = ssd_chunked(xs.astype(jnp.float32) * dt[..., None], A * dt, bm, cm)
    y = y + d_skip.astype(jnp.float32)[:, None] * xs.astype(jnp.float32)
    y = y.reshape(bsz, T, SSD_WIDTH) * jax.nn.silu(z.astype(jnp.float32))
    yg = y.reshape(bsz, T, SSD_GROUPS, SSD_WIDTH // SSD_GROUPS)
    yg = yg * lax.rsqrt(jnp.mean(yg * yg, axis=-1, keepdims=True) + EPS)
    y_ssd = (yg.reshape(bsz, T, SSD_WIDTH) * ssd_norm_w.astype(jnp.float32)).astype(h.dtype)
    heads = lambda t: t.reshape(bsz, T, ATT_HEADS, ATT_HEAD_DIM).transpose(0, 2, 1, 3)
    log_f = jax.nn.log_sigmoid(f_raw.astype(jnp.float32) + f_bias.astype(jnp.float32)).transpose(0, 2, 1)
    o = forgetting_attention(heads(q), heads(k), heads(v), log_f)
    y_att = o.transpose(0, 2, 1, 3).reshape(bsz, T, ATT_WIDTH).astype(h.dtype)
    return jnp.einsum("bte,ed->btd", jnp.concatenate([y_ssd, y_att], axis=-1), w_out)


def setup_inputs(seed: int = 0) -> dict:
    key = jax.random.key(seed)
    ks = jax.random.split(key, 16)
    f32 = jnp.float32
    L = DEPTH
    x = jax.random.normal(ks[0], (BATCH, SEQ, D_MODEL), f32)
    norm_mix_w = 1.0 + 0.01 * jax.random.normal(ks[1], (L, D_MODEL), f32)
    w_in = jax.random.normal(ks[2], (L, D_MODEL, IN_WIDTH), f32) * D_MODEL ** -0.5
    conv_w = jax.random.uniform(ks[3], (L, CONV_WIDTH, CONV_CH), f32, -1.0, 1.0) * CONV_WIDTH ** -0.5
    conv_b = 0.01 * jax.random.normal(ks[4], (L, CONV_CH), f32)
    dt0 = jnp.exp(jax.random.uniform(ks[5], (L, SSD_HEADS), f32, math.log(1e-3), math.log(1e-1)))
    dt_bias = dt0 + jnp.log(-jnp.expm1(-dt0))
    a_log = jnp.log(jax.random.uniform(ks[6], (L, SSD_HEADS), f32, 1.0, 16.0))
    d_skip = 1.0 + 0.01 * jax.random.normal(ks[7], (L, SSD_HEADS), f32)
    ssd_norm_w = 1.0 + 0.01 * jax.random.normal(ks[8], (L, SSD_WIDTH), f32)
    f_bias = jax.random.uniform(ks[9], (L, ATT_HEADS), f32, 1.0, 4.0)
    w_out = jax.random.normal(ks[10], (L, MIX_WIDTH, D_MODEL), f32) * MIX_WIDTH ** -0.5
    norm_mlp_w = 1.0 + 0.01 * jax.random.normal(ks[11], (L, D_MODEL), f32)
    w_up = jax.random.normal(ks[12], (L, D_MODEL, D_FF), f32) * D_MODEL ** -0.5
    w_down = jax.random.normal(ks[13], (L, D_FF, D_MODEL), f32) * D_FF ** -0.5
    norm_final_w = 1.0 + 0.01 * jax.random.normal(ks[14], (D_MODEL,), f32)
    return {"x": x, "norm_mix_w": norm_mix_w, "w_in": w_in, "conv_w": conv_w, "conv_b": conv_b,
            "dt_bias": dt_bias, "a_log": a_log, "d_skip": d_skip, "ssd_norm_w": ssd_norm_w,
            "f_bias": f_bias, "w_out": w_out, "norm_mlp_w": norm_mlp_w, "w_up": w_up,
            "w_down": w_down, "norm_final_w": norm_final_w}


def reference(x, norm_mix_w, w_in, conv_w, conv_b, dt_bias, a_log, d_skip, ssd_norm_w,
              f_bias, w_out, norm_mlp_w, w_up, w_down, norm_final_w):
    h = x
    for l in range(DEPTH):
        h = h + hybrid_mixer(rmsnorm(h, norm_mix_w[l]), w_in[l], conv_w[l], conv_b[l], dt_bias[l],
                             a_log[l], d_skip[l], ssd_norm_w[l], f_bias[l], w_out[l])
        u = jnp.square(jax.nn.relu(jnp.einsum("btd,df->btf", rmsnorm(h, norm_mlp_w[l]), w_up[l])))
        h = h + jnp.einsum("btf,fd->btd", u, w_down[l])
    return rmsnorm(h, norm_final_w)
```

```python
import functools

import numpy as np
import jax
import jax.numpy as jnp
from jax import lax
from jax.experimental import pallas as pl
from jax.experimental.pallas import tpu as pltpu

D_MODEL = 1024
SSD_HEADS = 16
SSD_HEAD_DIM = 64
SSD_WIDTH = SSD_HEADS * SSD_HEAD_DIM
SSD_GROUPS = 2
SSD_STATE = 128
CONV_WIDTH = 4
CHUNK = 128
CONV_CH = SSD_WIDTH + 2 * SSD_GROUPS * SSD_STATE
ATT_HEADS = 16
ATT_HEAD_DIM = 64
ATT_WIDTH = ATT_HEADS * ATT_HEAD_DIM
D_FF = 4 * D_MODEL
EPS = 1e-5

LANES = 128
SUBLANES = 8
HEADS_PER_LANE_TILE = LANES // SSD_HEAD_DIM
N_PAIRS = SSD_HEADS // HEADS_PER_LANE_TILE
BIAS_LANES_PER_HEAD = LANES // ATT_HEADS
VMEM_LIMIT_BYTES = 56 * 1024 * 1024

ROW_TILE = 512
ATT_TILE = 256

F32 = jnp.float32
BF16 = jnp.bfloat16

_NT = (((1,), (1,)), ((), ()))


def _dot(a, b):
    return jnp.dot(a, b, preferred_element_type=F32)


def _dot_nt(a, b):
    return lax.dot_general(a, b, _NT, preferred_element_type=F32)


def _softplus(x):
    return jnp.maximum(x, 0.0) + jnp.log1p(jnp.exp(-jnp.abs(x)))


def _silu(x):
    return x / (1.0 + jnp.exp(-x))


def _split3(v):
    hi = v.astype(BF16)
    r1 = v - hi.astype(F32)
    mid = r1.astype(BF16)
    lo = (r1 - mid.astype(F32)).astype(BF16)
    return hi, mid, lo


def _const_spec(shape):
    zeros = (0,) * len(shape)
    return pl.BlockSpec(shape, lambda *_: zeros, pipeline_mode=pl.Buffered(1))


def _inproj_kernel(x_ref, nw_ref, wz_ref, wxbc_ref, wq_ref, wk_ref, wv_ref, wdtf_ref, wdtft_ref,
                   z_ref, xbc_ref, q_ref, k_ref, v_ref, dtf_ref, dtft_ref):
    x = x_ref[...]
    ms = jnp.mean(x * x, axis=-1, keepdims=True)
    hb = ((x * lax.rsqrt(ms + EPS)) * nw_ref[...]).astype(BF16)
    z_ref[...] = _dot(hb, wz_ref[...]).astype(BF16)
    xbc_ref[...] = _dot(hb, wxbc_ref[...]).astype(BF16)
    q_ref[...] = _dot(hb, wq_ref[...]).astype(BF16)
    k_ref[...] = _dot(hb, wk_ref[...]).astype(BF16)
    v_ref[...] = _dot(hb, wv_ref[...]).astype(BF16)
    dtf_ref[...] = _dot(hb, wdtf_ref[...])
    dtft_ref[...] = _dot_nt(wdtft_ref[...], hb)


def _in_proj(x2, norm_w, wz, wxbc, wq, wk, wv, wdtf, wdtft):
    m = x2.shape[0]
    tm = ROW_TILE
    row = lambda width: pl.BlockSpec((tm, width), lambda i: (i, 0))
    out_shapes = (
        jax.ShapeDtypeStruct((m, SSD_WIDTH), BF16),
        jax.ShapeDtypeStruct((m, CONV_CH), BF16),
        jax.ShapeDtypeStruct((m, ATT_WIDTH), BF16),
        jax.ShapeDtypeStruct((m, ATT_WIDTH), BF16),
        jax.ShapeDtypeStruct((m, ATT_WIDTH), BF16),
        jax.ShapeDtypeStruct((m, LANES), F32),
        jax.ShapeDtypeStruct((2 * SSD_HEADS, m), F32),
    )
    return pl.pallas_call(
        _inproj_kernel,
        grid=(m // tm,),
        in_specs=[row(D_MODEL), _const_spec(norm_w.shape), _const_spec(wz.shape), _const_spec(wxbc.shape),
                  _const_spec(wq.shape), _const_spec(wk.shape), _const_spec(wv.shape),
                  _const_spec(wdtf.shape), _const_spec(wdtft.shape)],
        out_specs=(row(SSD_WIDTH), row(CONV_CH), row(ATT_WIDTH), row(ATT_WIDTH), row(ATT_WIDTH), row(LANES),
                   pl.BlockSpec((2 * SSD_HEADS, tm), lambda i: (0, i))),
        out_shape=out_shapes,
        compiler_params=pltpu.CompilerParams(dimension_semantics=("arbitrary",),
                                             vmem_limit_bytes=VMEM_LIMIT_BYTES),
        name="in_proj",
    )(x2, norm_w, wz, wxbc, wq, wk, wv, wdtf, wdtft)


def _ssd_kernel(xbc_ref, z_ref, dtf_ref, dtft_ref, convw_ref, convb_ref, cbias_ref, calog_ref,
                rbias_ref, ralog_ref, dskip_ref, nw_ref, tri3_ref, trit3_ref, pq_ref, pk_ref,
                oq_ref, ok_ref, y_ref, cq_ref, ck_ref, tail_ref, st_ref, carry_ref):
    L = CHUNK

    @pl.when(pl.program_id(1) == 0)
    def _():
        tail_ref[...] = jnp.zeros_like(tail_ref)
        st_ref[...] = jnp.zeros_like(st_ref)
        carry_ref[...] = jnp.zeros_like(carry_ref)

    u = xbc_ref[...].astype(F32)
    ext = jnp.concatenate([tail_ref[...], u], axis=0)
    w = convw_ref[...]
    acc = convb_ref[...] + u * w[CONV_WIDTH - 1:CONV_WIDTH, :]
    for back in range(1, CONV_WIDTH):
        shifted = pltpu.roll(ext, back, axis=0)[SUBLANES:SUBLANES + L, :]
        acc = acc + shifted * w[CONV_WIDTH - 1 - back:CONV_WIDTH - back, :]
    tail_ref[...] = u[L - SUBLANES:L, :]
    xc = _silu(acc)
    xs = xc[:, :SSD_WIDTH]
    xsb = xs.astype(BF16)
    bmat = xc[:, SSD_WIDTH:SSD_WIDTH + SSD_GROUPS * SSD_STATE]
    cmat = xc[:, SSD_WIDTH + SSD_GROUPS * SSD_STATE:]

    lane = lax.broadcasted_iota(jnp.int32, (L, LANES), 1)
    raw = dtf_ref[...] + cbias_ref[...]
    a_col = -jnp.exp(calog_ref[...]) * _softplus(raw)
    vcol = jnp.where(lane < SSD_HEADS, a_col, jnp.where(lane < 2 * SSD_HEADS, -_softplus(-raw), 0.0))
    cum = _dot(tri3_ref[...], jnp.concatenate(_split3(vcol), axis=0))

    row = lax.broadcasted_iota(jnp.int32, (2 * SSD_HEADS, L), 0)
    dt_t = _softplus(dtft_ref[...] + rbias_ref[...])
    a_t = jnp.where(row < SSD_HEADS, -jnp.exp(ralog_ref[...]) * dt_t, 0.0)
    acum_t = _dot(jnp.concatenate(_split3(a_t), axis=1), trit3_ref[...])
    last_t = acum_t[:, L - 1:L]
    w_t = jnp.exp(last_t - acum_t) * dt_t

    ri = lax.broadcasted_iota(jnp.int32, (L, L), 0)
    ci = lax.broadcasted_iota(jnp.int32, (L, L), 1)
    causal = ri >= ci

    cb, b_t, c_f = [], [], []
    for g in range(SSD_GROUPS):
        bg = bmat[:, g * SSD_STATE:(g + 1) * SSD_STATE]
        cg = cmat[:, g * SSD_STATE:(g + 1) * SSD_STATE]
        cb.append(_dot_nt(cg.astype(BF16), bg.astype(BF16)))
        b_t.append(bg.T)
        c_f.append(cg)

    g_mats, ce_mats, btw_mats, cdecay = [], [], [], []
    for h in range(SSD_HEADS):
        g = h // (SSD_HEADS // SSD_GROUPS)
        acol = jnp.broadcast_to(cum[:, h:h + 1], (L, L))
        arow = acum_t[h:h + 1, :]
        decay = jnp.where(causal, jnp.exp(acol - arow), 0.0)
        g_mats.append((cb[g] * decay * dt_t[h:h + 1, :]).astype(BF16))
        ce_mats.append((c_f[g] * jnp.exp(acol)).astype(BF16))
        btw_mats.append((b_t[g] * w_t[h:h + 1, :]).astype(BF16))
        cdecay.append(jnp.exp(jnp.broadcast_to(last_t[h:h + 1, :], (L, LANES))))

    lo_half = lane < SSD_HEAD_DIM
    st_mask = jnp.concatenate([lo_half, jnp.logical_not(lo_half)], axis=0)
    ys = []
    for i in range(N_PAIRS):
        h0, h1 = 2 * i, 2 * i + 1
        xp = xsb[:, i * LANES:(i + 1) * LANES]
        zero = jnp.zeros_like(xp)
        st = st_ref[i]
        rhs = jnp.concatenate([jnp.where(lo_half, xp, zero), jnp.where(lo_half, zero, xp),
                               st.astype(BF16)], axis=0)
        lhs = jnp.concatenate([g_mats[h0], g_mats[h1], ce_mats[h0], ce_mats[h1]], axis=1)
        ys.append(_dot(lhs, rhs))
        snew = _dot(jnp.concatenate([btw_mats[h0], btw_mats[h1]], axis=0), xp)
        cd = jnp.concatenate([cdecay[h0], cdecay[h1]], axis=0)
        st_ref[i] = jnp.where(st_mask, st * cd + snew, 0.0)

    y = jnp.concatenate(ys, axis=1) + dskip_ref[...] * xs
    y = y * _silu(z_ref[...].astype(F32))
    gw = SSD_WIDTH // SSD_GROUPS
    normed = []
    for g in range(SSD_GROUPS):
        yg = y[:, g * gw:(g + 1) * gw]
        normed.append(yg * lax.rsqrt(jnp.mean(yg * yg, axis=-1, keepdims=True) + EPS))
    y_ref[...] = (jnp.concatenate(normed, axis=1) * nw_ref[...]).astype(BF16)

    c_run = cum + carry_ref[0:1, :]
    in_f = jnp.logical_and(lane[0:1, :] >= SSD_HEADS, lane[0:1, :] < 2 * SSD_HEADS)
    carry_ref[0:1, :] = jnp.where(in_f, c_run[L - 1:L, :], 0.0)
    c3 = jnp.concatenate(_split3(c_run), axis=1)
    cq_ref[...] = (_dot(c3, pq_ref[...]) + oq_ref[...]).astype(BF16)
    ck_ref[...] = (_dot(c3, pk_ref[...]) + ok_ref[...]).astype(BF16)


def _ssd(xbc, z, dtf, dtft, consts, bsz, seq):
    m = xbc.shape[0]
    nc = seq // CHUNK
    row = lambda width: pl.BlockSpec((CHUNK, width), lambda b, c: (b * nc + c, 0))
    in_specs = [row(CONV_CH), row(SSD_WIDTH), row(LANES),
                pl.BlockSpec((2 * SSD_HEADS, CHUNK), lambda b, c: (0, b * nc + c))]
    in_specs += [_const_spec(a.shape) for a in consts]
    return pl.pallas_call(
        _ssd_kernel,
        grid=(bsz, nc),
        in_specs=in_specs,
        out_specs=(row(SSD_WIDTH), row(LANES), row(LANES)),
        out_shape=(jax.ShapeDtypeStruct((m, SSD_WIDTH), BF16),
                   jax.ShapeDtypeStruct((m, LANES), BF16),
                   jax.ShapeDtypeStruct((m, LANES), BF16)),
        scratch_shapes=[pltpu.VMEM((SUBLANES, CONV_CH), F32),
                        pltpu.VMEM((N_PAIRS, 2 * CHUNK, LANES), F32),
                        pltpu.VMEM((SUBLANES, LANES), F32)],
        compiler_params=pltpu.CompilerParams(dimension_semantics=("arbitrary", "arbitrary"),
                                             vmem_limit_bytes=VMEM_LIMIT_BYTES),
        name="ssd",
    )(xbc, z, dtf, dtft, *consts)


def _attn_kernel(q_ref, k_ref, v_ref, cq_ref, ck_ref, o_ref):
    t = ATT_TILE
    pair = pl.program_id(1)
    qi = pl.program_id(2)
    lane = lax.broadcasted_iota(jnp.int32, (t, LANES), 1)
    lo_half = lane < ATT_HEAD_DIM
    qs = q_ref[0] * jnp.asarray(ATT_HEAD_DIM ** -0.5, BF16)
    cqv = cq_ref[0]
    zero = jnp.zeros_like(qs)
    ri = lax.broadcasted_iota(jnp.int32, (t, t), 0)
    ci = lax.broadcasted_iota(jnp.int32, (t, t), 1)
    causal = ri >= ci

    outs = []
    for hh in range(HEADS_PER_LANE_TILE):
        head = pair * HEADS_PER_LANE_TILE + hh
        qm = jnp.where(lo_half, qs, zero) if hh == 0 else jnp.where(lo_half, zero, qs)
        cm = jnp.where((lane // BIAS_LANES_PER_HEAD) == head, cqv, zero)
        qx = jnp.concatenate([qm, cm], axis=1)

        def scores(j, qx=qx):
            start = pl.multiple_of(j * t, t)
            kx = jnp.concatenate([k_ref[0, pl.ds(start, t), :], ck_ref[0, pl.ds(start, t), :]], axis=1)
            return _dot_nt(qx, kx), start

        s, start = scores(qi)
        s = jnp.where(causal, s, -jnp.inf)
        m0 = jnp.max(s, axis=-1, keepdims=True)
        p = jnp.exp(s - m0)
        l0 = jnp.sum(p, axis=-1, keepdims=True)
        acc0 = _dot(p.astype(BF16), v_ref[0, pl.ds(start, t), :])

        def body(j, carry, scores=scores):
            m, l, acc = carry
            s, start = scores(j)
            m_new = jnp.maximum(m, jnp.max(s, axis=-1, keepdims=True))
            alpha = jnp.exp(m - m_new)
            p = jnp.exp(s - m_new)
            l = alpha * l + jnp.sum(p, axis=-1, keepdims=True)
            acc = alpha * acc + _dot(p.astype(BF16), v_ref[0, pl.ds(start, t), :])
            return m_new, l, acc

        _, l_fin, acc_fin = lax.fori_loop(0, qi, body, (m0, l0, acc0))
        outs.append(acc_fin / l_fin)

    o_ref[0] = jnp.where(lo_half, outs[0], outs[1]).astype(BF16)


def _attention(q3, k3, v3, cq3, ck3):
    bsz, seq, _ = q3.shape
    t = ATT_TILE
    return pl.pallas_call(
        _attn_kernel,
        grid=(bsz, N_PAIRS, seq // t),
        in_specs=[pl.BlockSpec((1, t, LANES), lambda b, p, i: (b, i, p)),
                  pl.BlockSpec((1, seq, LANES), lambda b, p, i: (b, 0, p)),
                  pl.BlockSpec((1, seq, LANES), lambda b, p, i: (b, 0, p)),
                  pl.BlockSpec((1, t, LANES), lambda b, p, i: (b, i, 0)),
                  pl.BlockSpec((1, seq, LANES), lambda b, p, i: (b, 0, 0))],
        out_specs=pl.BlockSpec((1, t, LANES), lambda b, p, i: (b, i, p)),
        out_shape=jax.ShapeDtypeStruct((bsz, seq, ATT_WIDTH), BF16),
        compiler_params=pltpu.CompilerParams(dimension_semantics=("arbitrary",) * 3,
                                             vmem_limit_bytes=VMEM_LIMIT_BYTES),
        name="fox_attn",
    )(q3, k3, v3, cq3, ck3)


def _out_mlp_kernel(x_ref, ys_ref, ya_ref, wos_ref, woa_ref, nmw_ref, wup_ref, wdn_ref, nfw_ref, o_ref):
    h1 = x_ref[...] + _dot(ys_ref[...], wos_ref[...]) + _dot(ya_ref[...], woa_ref[...])
    ms = jnp.mean(h1 * h1, axis=-1, keepdims=True)
    hb = ((h1 * lax.rsqrt(ms + EPS)) * nmw_ref[...]).astype(BF16)
    o_ref[...] = h1
    for c in range(D_FF // D_MODEL):
        u = _dot(hb, wup_ref[:, c * D_MODEL:(c + 1) * D_MODEL])
        u = jnp.square(jnp.maximum(u, 0.0))
        o_ref[...] += _dot(u.astype(BF16), wdn_ref[c * D_MODEL:(c + 1) * D_MODEL, :])
    acc = o_ref[...]
    ms2 = jnp.mean(acc * acc, axis=-1, keepdims=True)
    o_ref[...] = (acc * lax.rsqrt(ms2 + EPS)) * nfw_ref[...]


def _out_mlp(x2, ys, ya, wos, woa, nmw, wup, wdn, nfw):
    m = x2.shape[0]
    tm = ROW_TILE
    row = lambda: pl.BlockSpec((tm, D_MODEL), lambda i: (i, 0))
    consts = (wos, woa, nmw, wup, wdn, nfw)
    return pl.pallas_call(
        _out_mlp_kernel,
        grid=(m // tm,),
        in_specs=[row(), row(), row()] + [_const_spec(a.shape) for a in consts],
        out_specs=row(),
        out_shape=jax.ShapeDtypeStruct((m, D_MODEL), F32),
        compiler_params=pltpu.CompilerParams(dimension_semantics=("arbitrary",),
                                             vmem_limit_bytes=VMEM_LIMIT_BYTES),
        name="out_mlp",
    )(x2, ys, ya, *consts)


def _ssd_constants():
    t = np.arange(CHUNK)
    tri = (t[None, :] <= t[:, None]).astype(np.float32)
    tri3 = np.concatenate([tri, tri, tri], axis=1)
    trit3 = np.concatenate([tri.T, tri.T, tri.T], axis=0)
    pq = np.zeros((3 * LANES, LANES), np.float32)
    pk = np.zeros((3 * LANES, LANES), np.float32)
    oq = np.zeros((1, LANES), np.float32)
    ok = np.zeros((1, LANES), np.float32)
    for h in range(ATT_HEADS):
        base = h * BIAS_LANES_PER_HEAD
        for j in range(3):
            src = j * LANES + SSD_HEADS + h
            pq[src, base + j] = 1.0
            pk[src, base + 3 + j] = -1.0
            ok[0, base + j] = 1.0
            oq[0, base + 3 + j] = 1.0
    return (jnp.asarray(tri3, BF16), jnp.asarray(trit3, BF16), jnp.asarray(pq, BF16), jnp.asarray(pk, BF16),
            jnp.asarray(oq), jnp.asarray(ok))


def kernel(x, norm_mix_w, w_in, conv_w, conv_b, dt_bias, a_log, d_skip, ssd_norm_w, f_bias, w_out,
           norm_mlp_w, w_up, w_down, norm_final_w):
    bsz, seq, _ = x.shape
    m = bsz * seq
    assert norm_mix_w.shape[0] == 1, "single layer"
    assert m % ROW_TILE == 0 and seq % ATT_TILE == 0 and seq % CHUNK == 0
    x2 = x.reshape(m, D_MODEL)

    w = w_in[0]
    o_z, o_xbc = SSD_WIDTH, SSD_WIDTH + CONV_CH
    o_dt = o_xbc + SSD_HEADS
    o_q, o_k, o_v = o_dt + ATT_WIDTH, o_dt + 2 * ATT_WIDTH, o_dt + 3 * ATT_WIDTH
    wz = w[:, :o_z].astype(BF16)
    wxbc = w[:, o_z:o_xbc].astype(BF16)
    wq = w[:, o_dt:o_q].astype(BF16)
    wk = w[:, o_q:o_k].astype(BF16)
    wv = w[:, o_k:o_v].astype(BF16)
    w_dtf = jnp.concatenate([w[:, o_xbc:o_dt], w[:, o_v:]], axis=1)
    wdtf = jnp.pad(w_dtf, ((0, 0), (0, LANES - 2 * SSD_HEADS))).astype(BF16)
    wdtft = w_dtf.T.astype(BF16)

    z, xbc, q, k, v, dtf, dtft = _in_proj(x2, norm_mix_w[0][None, :], wz, wxbc, wq, wk, wv, wdtf, wdtft)

    pad_lanes = lambda a: jnp.pad(a, (0, LANES - a.shape[0]))[None, :]
    col_bias = pad_lanes(jnp.concatenate([dt_bias[0], f_bias[0]]))
    col_alog = pad_lanes(a_log[0])
    row_bias = jnp.broadcast_to(jnp.concatenate([dt_bias[0], f_bias[0]])[:, None], (2 * SSD_HEADS, CHUNK))
    row_alog = jnp.broadcast_to(jnp.pad(a_log[0], (0, SSD_HEADS))[:, None], (2 * SSD_HEADS, CHUNK))
    dskip = jnp.repeat(d_skip[0], SSD_HEAD_DIM)[None, :]
    consts = (conv_w[0], conv_b[0][None, :], col_bias, col_alog, row_bias, row_alog, dskip,
              ssd_norm_w[0][None, :]) + _ssd_constants()
    y_ssd, cq, ck = _ssd(xbc, z, dtf, dtft, consts, bsz, seq)

    to3 = lambda a: a.reshape(bsz, seq, a.shape[-1])
    y_att = _attention(to3(q), to3(k), to3(v), to3(cq), to3(ck)).reshape(m, ATT_WIDTH)

    wo = w_out[0].astype(BF16)
    out = _out_mlp(x2, y_ssd, y_att, wo[:SSD_WIDTH], wo[SSD_WIDTH:], norm_mlp_w[0][None, :],
                   w_up[0].astype(BF16), w_down[0].astype(BF16), norm_final_w[None, :])
    return out.reshape(bsz, seq, D_MODEL)
```

```python
import functools

import numpy as np
import jax
import jax.numpy as jnp
from jax import lax
from jax.experimental import pallas as pl
from jax.experimental.pallas import tpu as pltpu

D_MODEL = 1024
SSD_HEADS = 16
SSD_HEAD_DIM = 64
SSD_WIDTH = SSD_HEADS * SSD_HEAD_DIM
SSD_GROUPS = 2
SSD_STATE = 128
CONV_WIDTH = 4
CHUNK = 128
CONV_CH = SSD_WIDTH + 2 * SSD_GROUPS * SSD_STATE
ATT_HEADS = 16
ATT_HEAD_DIM = 64
ATT_WIDTH = ATT_HEADS * ATT_HEAD_DIM
D_FF = 4 * D_MODEL
EPS = 1e-5

LANES = 128
SUBLANES = 8
HEADS_PER_LANE_TILE = LANES // SSD_HEAD_DIM
N_PAIRS = SSD_HEADS // HEADS_PER_LANE_TILE
BIAS_LANES_PER_HEAD = LANES // ATT_HEADS
VMEM_LIMIT_BYTES = 56 * 1024 * 1024

ROW_TILE = 512
ATT_TILE = 512
ATT_QUERY_CHUNK = 256

F32 = jnp.float32
BF16 = jnp.bfloat16

_NT = (((1,), (1,)), ((), ()))


def _dot(a, b):
    return jnp.dot(a, b, preferred_element_type=F32)


def _dot_nt(a, b):
    return lax.dot_general(a, b, _NT, preferred_element_type=F32)


def _softplus(x):
    return jnp.maximum(x, 0.0) + jnp.log1p(jnp.exp(-jnp.abs(x)))


def _silu(x):
    return x / (1.0 + jnp.exp(-x))


def _split3(v):
    hi = v.astype(BF16)
    r1 = v - hi.astype(F32)
    mid = r1.astype(BF16)
    lo = (r1 - mid.astype(F32)).astype(BF16)
    return hi, mid, lo


def _const_spec(shape):
    zeros = (0,) * len(shape)
    return pl.BlockSpec(shape, lambda *_: zeros, pipeline_mode=pl.Buffered(1))


def _inproj_kernel(x_ref, nw_ref, wz_ref, wxbc_ref, wq_ref, wk_ref, wvt_ref, wdtf_ref, wdtft_ref,
                   z_ref, xbc_ref, q_ref, k_ref, vt_ref, dtf_ref, dtft_ref):
    x = x_ref[...]
    ms = jnp.mean(x * x, axis=-1, keepdims=True)
    hb = ((x * lax.rsqrt(ms + EPS)) * nw_ref[...]).astype(BF16)
    z_ref[...] = _dot(hb, wz_ref[...]).astype(BF16)
    xbc_ref[...] = _dot(hb, wxbc_ref[...]).astype(BF16)
    q_ref[...] = _dot(hb, wq_ref[...]).astype(BF16)
    k_ref[...] = _dot(hb, wk_ref[...]).astype(BF16)
    vt_ref[...] = _dot_nt(wvt_ref[...], hb).astype(BF16)
    dtf_ref[...] = _dot(hb, wdtf_ref[...])
    dtft_ref[...] = _dot_nt(wdtft_ref[...], hb)


def _in_proj(x2, norm_w, wz, wxbc, wq, wk, wvt, wdtf, wdtft):
    m = x2.shape[0]
    tm = ROW_TILE
    row = lambda width: pl.BlockSpec((tm, width), lambda i: (i, 0))
    out_shapes = (
        jax.ShapeDtypeStruct((m, SSD_WIDTH), BF16),
        jax.ShapeDtypeStruct((m, CONV_CH), BF16),
        jax.ShapeDtypeStruct((m, ATT_WIDTH), BF16),
        jax.ShapeDtypeStruct((m, ATT_WIDTH), BF16),
        jax.ShapeDtypeStruct((ATT_WIDTH, m), BF16),
        jax.ShapeDtypeStruct((m, LANES), F32),
        jax.ShapeDtypeStruct((2 * SSD_HEADS, m), F32),
    )
    return pl.pallas_call(
        _inproj_kernel,
        grid=(m // tm,),
        in_specs=[row(D_MODEL), _const_spec(norm_w.shape), _const_spec(wz.shape), _const_spec(wxbc.shape),
                  _const_spec(wq.shape), _const_spec(wk.shape), _const_spec(wvt.shape),
                  _const_spec(wdtf.shape), _const_spec(wdtft.shape)],
        out_specs=(row(SSD_WIDTH), row(CONV_CH), row(ATT_WIDTH), row(ATT_WIDTH),
                   pl.BlockSpec((ATT_WIDTH, tm), lambda i: (0, i)), row(LANES),
                   pl.BlockSpec((2 * SSD_HEADS, tm), lambda i: (0, i))),
        out_shape=out_shapes,
        compiler_params=pltpu.CompilerParams(dimension_semantics=("arbitrary",),
                                             vmem_limit_bytes=VMEM_LIMIT_BYTES),
        name="in_proj",
    )(x2, norm_w, wz, wxbc, wq, wk, wvt, wdtf, wdtft)


def _ssd_kernel(xbc_ref, z_ref, dtf_ref, dtft_ref, convw_ref, convb_ref, cbias_ref, calog_ref,
                rbias_ref, ralog_ref, dskip_ref, nw_ref, tri3_ref, trit3_ref, pq_ref, pk_ref,
                oq_ref, ok_ref, y_ref, cq_ref, ck_ref, tail_ref, st_ref, carry_ref):
    L = CHUNK

    @pl.when(pl.program_id(1) == 0)
    def _():
        tail_ref[...] = jnp.zeros_like(tail_ref)
        st_ref[...] = jnp.zeros_like(st_ref)
        carry_ref[...] = jnp.zeros_like(carry_ref)

    u = xbc_ref[...].astype(F32)
    ext = jnp.concatenate([tail_ref[...], u], axis=0)
    w = convw_ref[...]
    acc = convb_ref[...] + u * w[CONV_WIDTH - 1:CONV_WIDTH, :]
    for back in range(1, CONV_WIDTH):
        shifted = pltpu.roll(ext, back, axis=0)[SUBLANES:SUBLANES + L, :]
        acc = acc + shifted * w[CONV_WIDTH - 1 - back:CONV_WIDTH - back, :]
    tail_ref[...] = u[L - SUBLANES:L, :]
    xc = _silu(acc)
    xs = xc[:, :SSD_WIDTH]
    xsb = xs.astype(BF16)
    bmat = xc[:, SSD_WIDTH:SSD_WIDTH + SSD_GROUPS * SSD_STATE]
    cmat = xc[:, SSD_WIDTH + SSD_GROUPS * SSD_STATE:]

    lane = lax.broadcasted_iota(jnp.int32, (L, LANES), 1)
    raw = dtf_ref[...] + cbias_ref[...]
    a_col = -jnp.exp(calog_ref[...]) * _softplus(raw)
    vcol = jnp.where(lane < SSD_HEADS, a_col, jnp.where(lane < 2 * SSD_HEADS, -_softplus(-raw), 0.0))
    cum = _dot(tri3_ref[...], jnp.concatenate(_split3(vcol), axis=0))

    row = lax.broadcasted_iota(jnp.int32, (2 * SSD_HEADS, L), 0)
    dt_t = _softplus(dtft_ref[...] + rbias_ref[...])
    a_t = jnp.where(row < SSD_HEADS, -jnp.exp(ralog_ref[...]) * dt_t, 0.0)
    acum_t = _dot(jnp.concatenate(_split3(a_t), axis=1), trit3_ref[...])
    last_t = acum_t[:, L - 1:L]
    w_t = jnp.exp(last_t - acum_t) * dt_t

    ri = lax.broadcasted_iota(jnp.int32, (L, L), 0)
    ci = lax.broadcasted_iota(jnp.int32, (L, L), 1)
    causal = ri >= ci

    cb, b_t, c_f = [], [], []
    for g in range(SSD_GROUPS):
        bg = bmat[:, g * SSD_STATE:(g + 1) * SSD_STATE]
        cg = cmat[:, g * SSD_STATE:(g + 1) * SSD_STATE]
        cb.append(_dot_nt(cg.astype(BF16), bg.astype(BF16)))
        b_t.append(bg.T)
        c_f.append(cg)

    g_mats, ce_mats, btw_mats, cdecay = [], [], [], []
    for h in range(SSD_HEADS):
        g = h // (SSD_HEADS // SSD_GROUPS)
        acol = jnp.broadcast_to(cum[:, h:h + 1], (L, L))
        arow = acum_t[h:h + 1, :]
        decay = jnp.where(causal, jnp.exp(acol - arow), 0.0)
        g_mats.append((cb[g] * decay * dt_t[h:h + 1, :]).astype(BF16))
        ce_mats.append((c_f[g] * jnp.exp(acol)).astype(BF16))
        btw_mats.append((b_t[g] * w_t[h:h + 1, :]).astype(BF16))
        cdecay.append(jnp.exp(jnp.broadcast_to(last_t[h:h + 1, :], (L, LANES))))

    lo_half = lane < SSD_HEAD_DIM
    st_mask = jnp.concatenate([lo_half, jnp.logical_not(lo_half)], axis=0)
    ys = []
    for i in range(N_PAIRS):
        h0, h1 = 2 * i, 2 * i + 1
        xp = xsb[:, i * LANES:(i + 1) * LANES]
        zero = jnp.zeros_like(xp)
        st = st_ref[i]
        rhs = jnp.concatenate([jnp.where(lo_half, xp, zero), jnp.where(lo_half, zero, xp),
                               st.astype(BF16)], axis=0)
        lhs = jnp.concatenate([g_mats[h0], g_mats[h1], ce_mats[h0], ce_mats[h1]], axis=1)
        ys.append(_dot(lhs, rhs))
        snew = _dot(jnp.concatenate([btw_mats[h0], btw_mats[h1]], axis=0), xp)
        cd = jnp.concatenate([cdecay[h0], cdecay[h1]], axis=0)
        st_ref[i] = jnp.where(st_mask, st * cd + snew, 0.0)

    y = jnp.concatenate(ys, axis=1) + dskip_ref[...] * xs
    y = y * _silu(z_ref[...].astype(F32))
    gw = SSD_WIDTH // SSD_GROUPS
    normed = []
    for g in range(SSD_GROUPS):
        yg = y[:, g * gw:(g + 1) * gw]
        normed.append(yg * lax.rsqrt(jnp.mean(yg * yg, axis=-1, keepdims=True) + EPS))
    y_ref[...] = (jnp.concatenate(normed, axis=1) * nw_ref[...]).astype(BF16)

    c_run = cum + carry_ref[0:1, :]
    in_f = jnp.logical_and(lane[0:1, :] >= SSD_HEADS, lane[0:1, :] < 2 * SSD_HEADS)
    carry_ref[0:1, :] = jnp.where(in_f, c_run[L - 1:L, :], 0.0)
    c3 = jnp.concatenate(_split3(c_run), axis=1)
    cq_ref[...] = (_dot(c3, pq_ref[...]) + oq_ref[...]).astype(BF16)
    ck_ref[...] = (_dot(c3, pk_ref[...]) + ok_ref[...]).astype(BF16)


def _ssd(xbc, z, dtf, dtft, consts, bsz, seq):
    m = xbc.shape[0]
    nc = seq // CHUNK
    row = lambda width: pl.BlockSpec((CHUNK, width), lambda b, c: (b * nc + c, 0))
    in_specs = [row(CONV_CH), row(SSD_WIDTH), row(LANES),
                pl.BlockSpec((2 * SSD_HEADS, CHUNK), lambda b, c: (0, b * nc + c))]
    in_specs += [_const_spec(a.shape) for a in consts]
    return pl.pallas_call(
        _ssd_kernel,
        grid=(bsz, nc),
        in_specs=in_specs,
        out_specs=(row(SSD_WIDTH), row(LANES), row(LANES)),
        out_shape=(jax.ShapeDtypeStruct((m, SSD_WIDTH), BF16),
                   jax.ShapeDtypeStruct((m, LANES), BF16),
                   jax.ShapeDtypeStruct((m, LANES), BF16)),
        scratch_shapes=[pltpu.VMEM((SUBLANES, CONV_CH), F32),
                        pltpu.VMEM((N_PAIRS, 2 * CHUNK, LANES), F32),
                        pltpu.VMEM((SUBLANES, LANES), F32)],
        compiler_params=pltpu.CompilerParams(dimension_semantics=("arbitrary", "arbitrary"),
                                             vmem_limit_bytes=VMEM_LIMIT_BYTES),
        name="ssd",
    )(xbc, z, dtf, dtft, *consts)


def _attn_kernel(q_ref, k_ref, vt_ref, cq_ref, ck_ref, o_ref, qx_ref, m_ref, l_ref, acc_ref):
    t = ATT_TILE
    pair = pl.program_id(1)
    qi = pl.program_id(2)

    lane = lax.broadcasted_iota(jnp.int32, (t, LANES), 1)
    lo_half = lane < ATT_HEAD_DIM
    qs = q_ref[0] * jnp.asarray(ATT_HEAD_DIM ** -0.5, BF16)
    cqv = cq_ref[0]
    zero = jnp.zeros_like(qs)
    for hh in range(HEADS_PER_LANE_TILE):
        head = pair * HEADS_PER_LANE_TILE + hh
        qm = jnp.where(lo_half, qs, zero) if hh == 0 else jnp.where(lo_half, zero, qs)
        cm = jnp.where((lane // BIAS_LANES_PER_HEAD) == head, cqv, zero)
        qx_ref[hh * t:(hh + 1) * t, :] = jnp.concatenate([qm, cm], axis=1)

    cc = ATT_QUERY_CHUNK
    key_i = lax.broadcasted_iota(jnp.int32, (t, cc), 0)
    qry_i = lax.broadcasted_iota(jnp.int32, (t, cc), 1)

    def kv_tile(j, first):
        start = pl.multiple_of(j * t, t)
        kx = jnp.concatenate([k_ref[0, pl.ds(start, t), :], ck_ref[0, pl.ds(start, t), :]], axis=1)
        s_all = _dot_nt(kx, qx_ref[...])
        for c in range(HEADS_PER_LANE_TILE * t // cc):
            cols = slice(c * cc, (c + 1) * cc)
            hh = (c * cc) // t
            vh = vt_ref[hh * ATT_HEAD_DIM:(hh + 1) * ATT_HEAD_DIM, pl.ds(start, t)]
            s = s_all[:, cols]
            if first:
                s = jnp.where(key_i <= qry_i + (c * cc) % t, s, -jnp.inf)
                m_new = jnp.max(s, axis=0, keepdims=True)
                p = jnp.exp(s - m_new)
                l_ref[0:1, cols] = jnp.sum(p, axis=0, keepdims=True)
                acc_ref[:, cols] = _dot(vh, p.astype(BF16))
            else:
                m_old = m_ref[0:1, cols]
                m_new = jnp.maximum(m_old, jnp.max(s, axis=0, keepdims=True))
                alpha = jnp.exp(m_old - m_new)
                p = jnp.exp(s - m_new)
                l_ref[0:1, cols] = alpha * l_ref[0:1, cols] + jnp.sum(p, axis=0, keepdims=True)
                acc_ref[:, cols] = alpha * acc_ref[:, cols] + _dot(vh, p.astype(BF16))
            m_ref[0:1, cols] = m_new

    kv_tile(qi, True)

    def body(j, carry):
        kv_tile(j, False)
        return carry

    lax.fori_loop(0, qi, body, 0)

    o_t = jnp.concatenate([acc_ref[:, hh * t:(hh + 1) * t] / l_ref[0:1, hh * t:(hh + 1) * t]
                           for hh in range(HEADS_PER_LANE_TILE)], axis=0)
    o_ref[0] = o_t.T.astype(BF16)


def _attention(q3, k3, vt, cq3, ck3):
    bsz, seq, _ = q3.shape
    t = ATT_TILE
    cols = HEADS_PER_LANE_TILE * t
    return pl.pallas_call(
        _attn_kernel,
        grid=(bsz, N_PAIRS, seq // t),
        in_specs=[pl.BlockSpec((1, t, LANES), lambda b, p, i: (b, i, p)),
                  pl.BlockSpec((1, seq, LANES), lambda b, p, i: (b, 0, p)),
                  pl.BlockSpec((LANES, seq), lambda b, p, i: (p, b)),
                  pl.BlockSpec((1, t, LANES), lambda b, p, i: (b, i, 0)),
                  pl.BlockSpec((1, seq, LANES), lambda b, p, i: (b, 0, 0))],
        out_specs=pl.BlockSpec((1, t, LANES), lambda b, p, i: (b, i, p)),
        out_shape=jax.ShapeDtypeStruct((bsz, seq, ATT_WIDTH), BF16),
        scratch_shapes=[pltpu.VMEM((cols, 2 * LANES), BF16),
                        pltpu.VMEM((SUBLANES, cols), F32),
                        pltpu.VMEM((SUBLANES, cols), F32),
                        pltpu.VMEM((ATT_HEAD_DIM, cols), F32)],
        compiler_params=pltpu.CompilerParams(dimension_semantics=("arbitrary",) * 3,
                                             vmem_limit_bytes=VMEM_LIMIT_BYTES),
        name="fox_attn",
    )(q3, k3, vt, cq3, ck3)


def _out_mlp_kernel(x_ref, ys_ref, ya_ref, wos_ref, woa_ref, nmw_ref, wup_ref, wdn_ref, nfw_ref, o_ref):
    h1 = x_ref[...] + _dot(ys_ref[...], wos_ref[...]) + _dot(ya_ref[...], woa_ref[...])
    ms = jnp.mean(h1 * h1, axis=-1, keepdims=True)
    hb = ((h1 * lax.rsqrt(ms + EPS)) * nmw_ref[...]).astype(BF16)
    o_ref[...] = h1
    for c in range(D_FF // D_MODEL):
        u = _dot(hb, wup_ref[:, c * D_MODEL:(c + 1) * D_MODEL])
        u = jnp.square(jnp.maximum(u, 0.0))
        o_ref[...] += _dot(u.astype(BF16), wdn_ref[c * D_MODEL:(c + 1) * D_MODEL, :])
    acc = o_ref[...]
    ms2 = jnp.mean(acc * acc, axis=-1, keepdims=True)
    o_ref[...] = (acc * lax.rsqrt(ms2 + EPS)) * nfw_ref[...]


def _out_mlp(x2, ys, ya, wos, woa, nmw, wup, wdn, nfw):
    m = x2.shape[0]
    tm = ROW_TILE
    row = lambda: pl.BlockSpec((tm, D_MODEL), lambda i: (i, 0))
    consts = (wos, woa, nmw, wup, wdn, nfw)
    return pl.pallas_call(
        _out_mlp_kernel,
        grid=(m // tm,),
        in_specs=[row(), row(), row()] + [_const_spec(a.shape) for a in consts],
        out_specs=row(),
        out_shape=jax.ShapeDtypeStruct((m, D_MODEL), F32),
        compiler_params=pltpu.CompilerParams(dimension_semantics=("arbitrary",),
                                             vmem_limit_bytes=VMEM_LIMIT_BYTES),
        name="out_mlp",
    )(x2, ys, ya, *consts)


def _ssd_constants():
    t = np.arange(CHUNK)
    tri = (t[None, :] <= t[:, None]).astype(np.float32)
    tri3 = np.concatenate([tri, tri, tri], axis=1)
    trit3 = np.concatenate([tri.T, tri.T, tri.T], axis=0)
    pq = np.zeros((3 * LANES, LANES), np.float32)
    pk = np.zeros((3 * LANES, LANES), np.float32)
    oq = np.zeros((1, LANES), np.float32)
    ok = np.zeros((1, LANES), np.float32)
    for h in range(ATT_HEADS):
        base = h * BIAS_LANES_PER_HEAD
        for j in range(3):
            src = j * LANES + SSD_HEADS + h
            pq[src, base + j] = 1.0
            pk[src, base + 3 + j] = -1.0
            ok[0, base + j] = 1.0
            oq[0, base + 3 + j] = 1.0
    return (jnp.asarray(tri3, BF16), jnp.asarray(trit3, BF16), jnp.asarray(pq, BF16), jnp.asarray(pk, BF16),
            jnp.asarray(oq), jnp.asarray(ok))


def kernel(x, norm_mix_w, w_in, conv_w, conv_b, dt_bias, a_log, d_skip, ssd_norm_w, f_bias, w_out,
           norm_mlp_w, w_up, w_down, norm_final_w):
    bsz, seq, _ = x.shape
    m = bsz * seq
    assert norm_mix_w.shape[0] == 1, "single layer"
    assert m % ROW_TILE == 0 and seq % ATT_TILE == 0 and seq % CHUNK == 0
    x2 = x.reshape(m, D_MODEL)

    w = w_in[0]
    o_z, o_xbc = SSD_WIDTH, SSD_WIDTH + CONV_CH
    o_dt = o_xbc + SSD_HEADS
    o_q, o_k, o_v = o_dt + ATT_WIDTH, o_dt + 2 * ATT_WIDTH, o_dt + 3 * ATT_WIDTH
    wz = w[:, :o_z].astype(BF16)
    wxbc = w[:, o_z:o_xbc].astype(BF16)
    wq = w[:, o_dt:o_q].astype(BF16)
    wk = w[:, o_q:o_k].astype(BF16)
    wvt = w[:, o_k:o_v].T.astype(BF16)
    w_dtf = jnp.concatenate([w[:, o_xbc:o_dt], w[:, o_v:]], axis=1)
    wdtf = jnp.pad(w_dtf, ((0, 0), (0, LANES - 2 * SSD_HEADS))).astype(BF16)
    wdtft = w_dtf.T.astype(BF16)

    z, xbc, q, k, vt, dtf, dtft = _in_proj(x2, norm_mix_w[0][None, :], wz, wxbc, wq, wk, wvt, wdtf, wdtft)

    pad_lanes = lambda a: jnp.pad(a, (0, LANES - a.shape[0]))[None, :]
    col_bias = pad_lanes(jnp.concatenate([dt_bias[0], f_bias[0]]))
    col_alog = pad_lanes(a_log[0])
    row_bias = jnp.broadcast_to(jnp.concatenate([dt_bias[0], f_bias[0]])[:, None], (2 * SSD_HEADS, CHUNK))
    row_alog = jnp.broadcast_to(jnp.pad(a_log[0], (0, SSD_HEADS))[:, None], (2 * SSD_HEADS, CHUNK))
    dskip = jnp.repeat(d_skip[0], SSD_HEAD_DIM)[None, :]
    consts = (conv_w[0], conv_b[0][None, :], col_bias, col_alog, row_bias, row_alog, dskip,
              ssd_norm_w[0][None, :]) + _ssd_constants()
    y_ssd, cq, ck = _ssd(xbc, z, dtf, dtft, consts, bsz, seq)

    to3 = lambda a: a.reshape(bsz, seq, a.shape[-1])
    y_att = _attention(to3(q), to3(k), vt, to3(cq), to3(ck)).reshape(m, ATT_WIDTH)

    wo = w_out[0].astype(BF16)
    out = _out_mlp(x2, y_ssd, y_att, wo[:SSD_WIDTH], wo[SSD_WIDTH:], norm_mlp_w[0][None, :],
                   w_up[0].astype(BF16), w_down[0].astype(BF16), norm_final_w[None, :])
    return out.reshape(bsz, seq, D_MODEL)
```

```python
import functools

import numpy as np
import jax
import jax.numpy as jnp
from jax import lax
from jax.experimental import pallas as pl
from jax.experimental.pallas import tpu as pltpu

D_MODEL = 1024
SSD_HEADS = 16
SSD_HEAD_DIM = 64
SSD_WIDTH = SSD_HEADS * SSD_HEAD_DIM
SSD_GROUPS = 2
SSD_STATE = 128
CONV_WIDTH = 4
CHUNK = 128
CONV_CH = SSD_WIDTH + 2 * SSD_GROUPS * SSD_STATE
ATT_HEADS = 16
ATT_HEAD_DIM = 64
ATT_WIDTH = ATT_HEADS * ATT_HEAD_DIM
D_FF = 4 * D_MODEL
EPS = 1e-5

LANES = 128
SUBLANES = 8
HEADS_PER_LANE_TILE = LANES // SSD_HEAD_DIM
N_PAIRS = SSD_HEADS // HEADS_PER_LANE_TILE
BIAS_LANES_PER_HEAD = LANES // ATT_HEADS
VMEM_LIMIT_BYTES = 56 * 1024 * 1024

ROW_TILE = 512
ATT_TILE = 512
ATT_QUERY_CHUNK = 256
ATT_DENOM_ROWS = 16
LOG2E = 1.4426950408889634
QK_SCALE = LOG2E * ATT_HEAD_DIM ** -0.5

F32 = jnp.float32
BF16 = jnp.bfloat16

_NT = (((1,), (1,)), ((), ()))


def _dot(a, b):
    return jnp.dot(a, b, preferred_element_type=F32)


def _dot_nt(a, b):
    return lax.dot_general(a, b, _NT, preferred_element_type=F32)


def _softplus(x):
    return jnp.maximum(x, 0.0) + jnp.log1p(jnp.exp(-jnp.abs(x)))


def _silu(x):
    return x / (1.0 + jnp.exp(-x))


def _split3(v):
    hi = v.astype(BF16)
    r1 = v - hi.astype(F32)
    mid = r1.astype(BF16)
    lo = (r1 - mid.astype(F32)).astype(BF16)
    return hi, mid, lo


def _const_spec(shape):
    zeros = (0,) * len(shape)
    return pl.BlockSpec(shape, lambda *_: zeros, pipeline_mode=pl.Buffered(1))


def _inproj_kernel(x_ref, nw_ref, wz_ref, wxbc_ref, wq_ref, wk_ref, wvt_ref, wdtf_ref, wdtft_ref,
                   z_ref, xbc_ref, q_ref, k_ref, vt_ref, dtf_ref, dtft_ref):
    x = x_ref[...]
    ms = jnp.mean(x * x, axis=-1, keepdims=True)
    hb = ((x * lax.rsqrt(ms + EPS)) * nw_ref[...]).astype(BF16)
    z_ref[...] = _dot(hb, wz_ref[...]).astype(BF16)
    xbc_ref[...] = _dot(hb, wxbc_ref[...]).astype(BF16)
    q_ref[...] = (_dot(hb, wq_ref[...]) * QK_SCALE).astype(BF16)
    k_ref[...] = _dot(hb, wk_ref[...]).astype(BF16)
    vt_ref[...] = _dot_nt(wvt_ref[...], hb).astype(BF16)
    dtf_ref[...] = _dot(hb, wdtf_ref[...])
    dtft_ref[...] = _dot_nt(wdtft_ref[...], hb)


def _in_proj(x2, norm_w, wz, wxbc, wq, wk, wvt, wdtf, wdtft):
    m = x2.shape[0]
    tm = ROW_TILE
    row = lambda width: pl.BlockSpec((tm, width), lambda i: (i, 0))
    out_shapes = (
        jax.ShapeDtypeStruct((m, SSD_WIDTH), BF16),
        jax.ShapeDtypeStruct((m, CONV_CH), BF16),
        jax.ShapeDtypeStruct((m, ATT_WIDTH), BF16),
        jax.ShapeDtypeStruct((m, ATT_WIDTH), BF16),
        jax.ShapeDtypeStruct((ATT_WIDTH, m), BF16),
        jax.ShapeDtypeStruct((m, LANES), F32),
        jax.ShapeDtypeStruct((2 * SSD_HEADS, m), F32),
    )
    return pl.pallas_call(
        _inproj_kernel,
        grid=(m // tm,),
        in_specs=[row(D_MODEL), _const_spec(norm_w.shape), _const_spec(wz.shape), _const_spec(wxbc.shape),
                  _const_spec(wq.shape), _const_spec(wk.shape), _const_spec(wvt.shape),
                  _const_spec(wdtf.shape), _const_spec(wdtft.shape)],
        out_specs=(row(SSD_WIDTH), row(CONV_CH), row(ATT_WIDTH), row(ATT_WIDTH),
                   pl.BlockSpec((ATT_WIDTH, tm), lambda i: (0, i)), row(LANES),
                   pl.BlockSpec((2 * SSD_HEADS, tm), lambda i: (0, i))),
        out_shape=out_shapes,
        compiler_params=pltpu.CompilerParams(dimension_semantics=("arbitrary",),
                                             vmem_limit_bytes=VMEM_LIMIT_BYTES),
        name="in_proj",
    )(x2, norm_w, wz, wxbc, wq, wk, wvt, wdtf, wdtft)


def _ssd_kernel(xbc_ref, z_ref, dtf_ref, dtft_ref, convw_ref, convb_ref, cbias_ref, calog_ref,
                rbias_ref, ralog_ref, dskip_ref, nw_ref, tri3_ref, trit3_ref, pq_ref, pk_ref,
                oq_ref, ok_ref, y_ref, cq_ref, ck_ref, tail_ref, st_ref, carry_ref):
    L = CHUNK

    @pl.when(pl.program_id(1) == 0)
    def _():
        tail_ref[...] = jnp.zeros_like(tail_ref)
        st_ref[...] = jnp.zeros_like(st_ref)
        carry_ref[...] = jnp.zeros_like(carry_ref)

    u = xbc_ref[...].astype(F32)
    ext = jnp.concatenate([tail_ref[...], u], axis=0)
    w = convw_ref[...]
    acc = convb_ref[...] + u * w[CONV_WIDTH - 1:CONV_WIDTH, :]
    for back in range(1, CONV_WIDTH):
        shifted = pltpu.roll(ext, back, axis=0)[SUBLANES:SUBLANES + L, :]
        acc = acc + shifted * w[CONV_WIDTH - 1 - back:CONV_WIDTH - back, :]
    tail_ref[...] = u[L - SUBLANES:L, :]
    xc = _silu(acc)
    xs = xc[:, :SSD_WIDTH]
    xsb = xs.astype(BF16)
    bmat = xc[:, SSD_WIDTH:SSD_WIDTH + SSD_GROUPS * SSD_STATE]
    cmat = xc[:, SSD_WIDTH + SSD_GROUPS * SSD_STATE:]

    lane = lax.broadcasted_iota(jnp.int32, (L, LANES), 1)
    raw = dtf_ref[...] + cbias_ref[...]
    a_col = -jnp.exp(calog_ref[...]) * _softplus(raw)
    vcol = jnp.where(lane < SSD_HEADS, a_col, jnp.where(lane < 2 * SSD_HEADS, -_softplus(-raw), 0.0))
    cum = _dot(tri3_ref[...], jnp.concatenate(_split3(vcol), axis=0))

    row = lax.broadcasted_iota(jnp.int32, (2 * SSD_HEADS, L), 0)
    dt_t = _softplus(dtft_ref[...] + rbias_ref[...])
    a_t = jnp.where(row < SSD_HEADS, -jnp.exp(ralog_ref[...]) * dt_t, 0.0)
    acum_t = _dot(jnp.concatenate(_split3(a_t), axis=1), trit3_ref[...])
    last_t = acum_t[:, L - 1:L]
    w_t = jnp.exp(last_t - acum_t) * dt_t

    ri = lax.broadcasted_iota(jnp.int32, (L, L), 0)
    ci = lax.broadcasted_iota(jnp.int32, (L, L), 1)
    causal = ri >= ci

    cb, b_t, c_f = [], [], []
    for g in range(SSD_GROUPS):
        bg = bmat[:, g * SSD_STATE:(g + 1) * SSD_STATE]
        cg = cmat[:, g * SSD_STATE:(g + 1) * SSD_STATE]
        cb.append(_dot_nt(cg.astype(BF16), bg.astype(BF16)))
        b_t.append(bg.T)
        c_f.append(cg)

    g_mats, ce_mats, btw_mats, cdecay = [], [], [], []
    for h in range(SSD_HEADS):
        g = h // (SSD_HEADS // SSD_GROUPS)
        acol = jnp.broadcast_to(cum[:, h:h + 1], (L, L))
        arow = acum_t[h:h + 1, :]
        decay = jnp.where(causal, jnp.exp(acol - arow), 0.0)
        g_mats.append((cb[g] * decay * dt_t[h:h + 1, :]).astype(BF16))
        ce_mats.append((c_f[g] * jnp.exp(acol)).astype(BF16))
        btw_mats.append((b_t[g] * w_t[h:h + 1, :]).astype(BF16))
        cdecay.append(jnp.exp(jnp.broadcast_to(last_t[h:h + 1, :], (L, LANES))))

    lo_half = lane < SSD_HEAD_DIM
    st_mask = jnp.concatenate([lo_half, jnp.logical_not(lo_half)], axis=0)
    ys = []
    for i in range(N_PAIRS):
        h0, h1 = 2 * i, 2 * i + 1
        xp = xsb[:, i * LANES:(i + 1) * LANES]
        zero = jnp.zeros_like(xp)
        st = st_ref[i]
        rhs = jnp.concatenate([jnp.where(lo_half, xp, zero), jnp.where(lo_half, zero, xp),
                               st.astype(BF16)], axis=0)
        lhs = jnp.concatenate([g_mats[h0], g_mats[h1], ce_mats[h0], ce_mats[h1]], axis=1)
        ys.append(_dot(lhs, rhs))
        snew = _dot(jnp.concatenate([btw_mats[h0], btw_mats[h1]], axis=0), xp)
        cd = jnp.concatenate([cdecay[h0], cdecay[h1]], axis=0)
        st_ref[i] = jnp.where(st_mask, st * cd + snew, 0.0)

    y = jnp.concatenate(ys, axis=1) + dskip_ref[...] * xs
    y = y * _silu(z_ref[...].astype(F32))
    gw = SSD_WIDTH // SSD_GROUPS
    normed = []
    for g in range(SSD_GROUPS):
        yg = y[:, g * gw:(g + 1) * gw]
        normed.append(yg * lax.rsqrt(jnp.mean(yg * yg, axis=-1, keepdims=True) + EPS))
    y_ref[...] = (jnp.concatenate(normed, axis=1) * nw_ref[...]).astype(BF16)

    c_run = cum + carry_ref[0:1, :]
    in_f = jnp.logical_and(lane[0:1, :] >= SSD_HEADS, lane[0:1, :] < 2 * SSD_HEADS)
    carry_ref[0:1, :] = jnp.where(in_f, c_run[L - 1:L, :], 0.0)
    c3 = jnp.concatenate(_split3(c_run * LOG2E), axis=1)
    cq_ref[...] = (_dot(c3, pq_ref[...]) + oq_ref[...]).astype(BF16)
    ck_ref[...] = (_dot(c3, pk_ref[...]) + ok_ref[...]).astype(BF16)


def _ssd(xbc, z, dtf, dtft, consts, bsz, seq):
    m = xbc.shape[0]
    nc = seq // CHUNK
    row = lambda width: pl.BlockSpec((CHUNK, width), lambda b, c: (b * nc + c, 0))
    in_specs = [row(CONV_CH), row(SSD_WIDTH), row(LANES),
                pl.BlockSpec((2 * SSD_HEADS, CHUNK), lambda b, c: (0, b * nc + c))]
    in_specs += [_const_spec(a.shape) for a in consts]
    return pl.pallas_call(
        _ssd_kernel,
        grid=(bsz, nc),
        in_specs=in_specs,
        out_specs=(row(SSD_WIDTH), row(LANES), row(LANES)),
        out_shape=(jax.ShapeDtypeStruct((m, SSD_WIDTH), BF16),
                   jax.ShapeDtypeStruct((m, LANES), BF16),
                   jax.ShapeDtypeStruct((m, LANES), BF16)),
        scratch_shapes=[pltpu.VMEM((SUBLANES, CONV_CH), F32),
                        pltpu.VMEM((N_PAIRS, 2 * CHUNK, LANES), F32),
                        pltpu.VMEM((SUBLANES, LANES), F32)],
        compiler_params=pltpu.CompilerParams(dimension_semantics=("arbitrary", "arbitrary"),
                                             vmem_limit_bytes=VMEM_LIMIT_BYTES),
        name="ssd",
    )(xbc, z, dtf, dtft, *consts)


def _attn_kernel(q_ref, k_ref, vt_ref, cq_ref, ck_ref, o_ref, qx_ref, m_ref, acc_ref):
    t = ATT_TILE
    pair = pl.program_id(1)
    qi = pl.program_id(2)

    lane = lax.broadcasted_iota(jnp.int32, (t, LANES), 1)
    lo_half = lane < ATT_HEAD_DIM
    qs = q_ref[0]
    cqv = cq_ref[0]
    zero = jnp.zeros_like(qs)
    for hh in range(HEADS_PER_LANE_TILE):
        head = pair * HEADS_PER_LANE_TILE + hh
        qm = jnp.where(lo_half, qs, zero) if hh == 0 else jnp.where(lo_half, zero, qs)
        cm = jnp.where((lane // BIAS_LANES_PER_HEAD) == head, cqv, zero)
        qx_ref[hh * t:(hh + 1) * t, :] = jnp.concatenate([qm, cm], axis=1)

    cc = ATT_QUERY_CHUNK

    def ones_rows(nk):
        return jnp.where(lax.broadcasted_iota(jnp.int32, (ATT_DENOM_ROWS, nk), 0) == 0, 1.0, 0.0).astype(BF16)

    def scores(j):
        start = pl.multiple_of(j * t, t)
        kx = jnp.concatenate([k_ref[0, pl.ds(start, t), :], ck_ref[0, pl.ds(start, t), :]], axis=1)
        return _dot_nt(kx, qx_ref[...])

    def softmax_pv(s_all, j, first):
        start = pl.multiple_of(j * t, t)
        for c in range(HEADS_PER_LANE_TILE * t // cc):
            cols = slice(c * cc, (c + 1) * cc)
            hh = (c * cc) // t
            off = (c * cc) % t
            nk = off + cc if first else t
            vh = jnp.concatenate([vt_ref[hh * ATT_HEAD_DIM:(hh + 1) * ATT_HEAD_DIM, pl.ds(start, nk)],
                                  ones_rows(nk)], axis=0)
            s = s_all[:nk, cols]
            if first:
                key_i = lax.broadcasted_iota(jnp.int32, (nk, cc), 0)
                qry_i = lax.broadcasted_iota(jnp.int32, (nk, cc), 1)
                s = jnp.where(key_i <= qry_i + off, s, -jnp.inf)
                m_new = jnp.max(s, axis=0, keepdims=True)
                p = jnp.exp2(s - m_new)
                acc_ref[:, cols] = _dot(vh, p.astype(BF16))
            else:
                m_old = m_ref[0:1, cols]
                m_new = jnp.maximum(m_old, jnp.max(s, axis=0, keepdims=True))
                alpha = jnp.exp2(m_old - m_new)
                p = jnp.exp2(s - m_new)
                acc_ref[:, cols] = alpha * acc_ref[:, cols] + _dot(vh, p.astype(BF16))
            m_ref[0:1, cols] = m_new

    softmax_pv(scores(qi), qi, True)

    def body(j, carry):
        softmax_pv(scores(j), j, False)
        return carry

    lax.fori_loop(0, qi, body, 0)

    d = ATT_HEAD_DIM
    o_t = jnp.concatenate([acc_ref[0:d, hh * t:(hh + 1) * t] / acc_ref[d:d + 1, hh * t:(hh + 1) * t]
                           for hh in range(HEADS_PER_LANE_TILE)], axis=0)
    o_ref[0] = o_t.T.astype(BF16)


def _attention(q3, k3, vt, cq3, ck3):
    bsz, seq, _ = q3.shape
    t = ATT_TILE
    cols = HEADS_PER_LANE_TILE * t
    return pl.pallas_call(
        _attn_kernel,
        grid=(bsz, N_PAIRS, seq // t),
        in_specs=[pl.BlockSpec((1, t, LANES), lambda b, p, i: (b, i, p)),
                  pl.BlockSpec((1, seq, LANES), lambda b, p, i: (b, 0, p)),
                  pl.BlockSpec((LANES, seq), lambda b, p, i: (p, b)),
                  pl.BlockSpec((1, t, LANES), lambda b, p, i: (b, i, 0)),
                  pl.BlockSpec((1, seq, LANES), lambda b, p, i: (b, 0, 0))],
        out_specs=pl.BlockSpec((1, t, LANES), lambda b, p, i: (b, i, p)),
        out_shape=jax.ShapeDtypeStruct((bsz, seq, ATT_WIDTH), BF16),
        scratch_shapes=[pltpu.VMEM((cols, 2 * LANES), BF16),
                        pltpu.VMEM((SUBLANES, cols), F32),
                        pltpu.VMEM((ATT_HEAD_DIM + ATT_DENOM_ROWS, cols), F32)],
        compiler_params=pltpu.CompilerParams(dimension_semantics=("arbitrary",) * 3,
                                             vmem_limit_bytes=VMEM_LIMIT_BYTES),
        name="fox_attn",
    )(q3, k3, vt, cq3, ck3)


def _out_mlp_kernel(x_ref, ys_ref, ya_ref, wos_ref, woa_ref, nmw_ref, wup_ref, wdn_ref, nfw_ref, o_ref):
    h1 = x_ref[...] + _dot(ys_ref[...], wos_ref[...]) + _dot(ya_ref[...], woa_ref[...])
    ms = jnp.mean(h1 * h1, axis=-1, keepdims=True)
    hb = ((h1 * lax.rsqrt(ms + EPS)) * nmw_ref[...]).astype(BF16)
    o_ref[...] = h1
    for c in range(D_FF // D_MODEL):
        u = _dot(hb, wup_ref[:, c * D_MODEL:(c + 1) * D_MODEL])
        u = jnp.square(jnp.maximum(u, 0.0))
        o_ref[...] += _dot(u.astype(BF16), wdn_ref[c * D_MODEL:(c + 1) * D_MODEL, :])
    acc = o_ref[...]
    ms2 = jnp.mean(acc * acc, axis=-1, keepdims=True)
    o_ref[...] = (acc * lax.rsqrt(ms2 + EPS)) * nfw_ref[...]


def _out_mlp(x2, ys, ya, wos, woa, nmw, wup, wdn, nfw):
    m = x2.shape[0]
    tm = ROW_TILE
    row = lambda: pl.BlockSpec((tm, D_MODEL), lambda i: (i, 0))
    consts = (wos, woa, nmw, wup, wdn, nfw)
    return pl.pallas_call(
        _out_mlp_kernel,
        grid=(m // tm,),
        in_specs=[row(), row(), row()] + [_const_spec(a.shape) for a in consts],
        out_specs=row(),
        out_shape=jax.ShapeDtypeStruct((m, D_MODEL), F32),
        compiler_params=pltpu.CompilerParams(dimension_semantics=("arbitrary",),
                                             vmem_limit_bytes=VMEM_LIMIT_BYTES),
        name="out_mlp",
    )(x2, ys, ya, *consts)


def _ssd_constants():
    t = np.arange(CHUNK)
    tri = (t[None, :] <= t[:, None]).astype(np.float32)
    tri3 = np.concatenate([tri, tri, tri], axis=1)
    trit3 = np.concatenate([tri.T, tri.T, tri.T], axis=0)
    pq = np.zeros((3 * LANES, LANES), np.float32)
    pk = np.zeros((3 * LANES, LANES), np.float32)
    oq = np.zeros((1, LANES), np.float32)
    ok = np.zeros((1, LANES), np.float32)
    for h in range(ATT_HEADS):
        base = h * BIAS_LANES_PER_HEAD
        for j in range(3):
            src = j * LANES + SSD_HEADS + h
            pq[src, base + j] = 1.0
            pk[src, base + 3 + j] = -1.0
            ok[0, base + j] = 1.0
            oq[0, base + 3 + j] = 1.0
    return (jnp.asarray(tri3, BF16), jnp.asarray(trit3, BF16), jnp.asarray(pq, BF16), jnp.asarray(pk, BF16),
            jnp.asarray(oq), jnp.asarray(ok))


def kernel(x, norm_mix_w, w_in, conv_w, conv_b, dt_bias, a_log, d_skip, ssd_norm_w, f_bias, w_out,
           norm_mlp_w, w_up, w_down, norm_final_w):
    bsz, seq, _ = x.shape
    m = bsz * seq
    assert norm_mix_w.shape[0] == 1, "single layer"
    assert m % ROW_TILE == 0 and seq % ATT_TILE == 0 and seq % CHUNK == 0
    x2 = x.reshape(m, D_MODEL)

    w = w_in[0]
    o_z, o_xbc = SSD_WIDTH, SSD_WIDTH + CONV_CH
    o_dt = o_xbc + SSD_HEADS
    o_q, o_k, o_v = o_dt + ATT_WIDTH, o_dt + 2 * ATT_WIDTH, o_dt + 3 * ATT_WIDTH
    wz = w[:, :o_z].astype(BF16)
    wxbc = w[:, o_z:o_xbc].astype(BF16)
    wq = w[:, o_dt:o_q].astype(BF16)
    wk = w[:, o_q:o_k].astype(BF16)
    wvt = w[:, o_k:o_v].T.astype(BF16)
    w_dtf = jnp.concatenate([w[:, o_xbc:o_dt], w[:, o_v:]], axis=1)
    wdtf = jnp.pad(w_dtf, ((0, 0), (0, LANES - 2 * SSD_HEADS))).astype(BF16)
    wdtft = w_dtf.T.astype(BF16)

    z, xbc, q, k, vt, dtf, dtft = _in_proj(x2, norm_mix_w[0][None, :], wz, wxbc, wq, wk, wvt, wdtf, wdtft)

    pad_lanes = lambda a: jnp.pad(a, (0, LANES - a.shape[0]))[None, :]
    col_bias = pad_lanes(jnp.concatenate([dt_bias[0], f_bias[0]]))
    col_alog = pad_lanes(a_log[0])
    row_bias = jnp.broadcast_to(jnp.concatenate([dt_bias[0], f_bias[0]])[:, None], (2 * SSD_HEADS, CHUNK))
    row_alog = jnp.broadcast_to(jnp.pad(a_log[0], (0, SSD_HEADS))[:, None], (2 * SSD_HEADS, CHUNK))
    dskip = jnp.repeat(d_skip[0], SSD_HEAD_DIM)[None, :]
    consts = (conv_w[0], conv_b[0][None, :], col_bias, col_alog, row_bias, row_alog, dskip,
              ssd_norm_w[0][None, :]) + _ssd_constants()
    y_ssd, cq, ck = _ssd(xbc, z, dtf, dtft, consts, bsz, seq)

    to3 = lambda a: a.reshape(bsz, seq, a.shape[-1])
    y_att = _attention(to3(q), to3(k), vt, to3(cq), to3(ck)).reshape(m, ATT_WIDTH)

    wo = w_out[0].astype(BF16)
    out = _out_mlp(x2, y_ssd, y_att, wo[:SSD_WIDTH], wo[SSD_WIDTH:], norm_mlp_w[0][None, :],
                   w_up[0].astype(BF16), w_down[0].astype(BF16), norm_final_w[None, :])
    return out.reshape(bsz, seq, D_MODEL)
```

```python
import functools

import numpy as np
import jax
import jax.numpy as jnp
from jax import lax
from jax.experimental import pallas as pl
from jax.experimental.pallas import tpu as pltpu

D_MODEL = 1024
SSD_HEADS = 16
SSD_HEAD_DIM = 64
SSD_WIDTH = SSD_HEADS * SSD_HEAD_DIM
SSD_GROUPS = 2
SSD_STATE = 128
CONV_WIDTH = 4
CHUNK = 128
CONV_CH = SSD_WIDTH + 2 * SSD_GROUPS * SSD_STATE
ATT_HEADS = 16
ATT_HEAD_DIM = 64
ATT_WIDTH = ATT_HEADS * ATT_HEAD_DIM
D_FF = 4 * D_MODEL
EPS = 1e-5

LANES = 128
SUBLANES = 8
HEADS_PER_LANE_TILE = LANES // SSD_HEAD_DIM
N_PAIRS = SSD_HEADS // HEADS_PER_LANE_TILE
BIAS_LANES_PER_HEAD = LANES // ATT_HEADS
VMEM_LIMIT_BYTES = 56 * 1024 * 1024

ROW_TILE = 512
ATT_TILE = 1024
ATT_QUERY_CHUNK = 256
ATT_DENOM_ROWS = 16
LOG2E = 1.4426950408889634
QK_SCALE = LOG2E * ATT_HEAD_DIM ** -0.5

F32 = jnp.float32
BF16 = jnp.bfloat16

_NT = (((1,), (1,)), ((), ()))


def _dot(a, b):
    return jnp.dot(a, b, preferred_element_type=F32)


def _dot_nt(a, b):
    return lax.dot_general(a, b, _NT, preferred_element_type=F32)


def _softplus(x):
    return jnp.maximum(x, 0.0) + jnp.log1p(jnp.exp(-jnp.abs(x)))


def _silu(x):
    return x / (1.0 + jnp.exp(-x))


def _split3(v):
    hi = v.astype(BF16)
    r1 = v - hi.astype(F32)
    mid = r1.astype(BF16)
    lo = (r1 - mid.astype(F32)).astype(BF16)
    return hi, mid, lo


def _const_spec(shape):
    zeros = (0,) * len(shape)
    return pl.BlockSpec(shape, lambda *_: zeros, pipeline_mode=pl.Buffered(1))


def _inproj_kernel(x_ref, nw_ref, wz_ref, wxbc_ref, wq_ref, wk_ref, wvt_ref, wdtf_ref, wdtft_ref,
                   z_ref, xbc_ref, q_ref, k_ref, vt_ref, dtf_ref, dtft_ref):
    x = x_ref[...]
    ms = jnp.mean(x * x, axis=-1, keepdims=True)
    hb = ((x * lax.rsqrt(ms + EPS)) * nw_ref[...]).astype(BF16)
    z_ref[...] = _dot(hb, wz_ref[...]).astype(BF16)
    xbc_ref[...] = _dot(hb, wxbc_ref[...]).astype(BF16)
    q_ref[...] = (_dot(hb, wq_ref[...]) * QK_SCALE).astype(BF16)
    k_ref[...] = _dot(hb, wk_ref[...]).astype(BF16)
    vt_ref[...] = _dot_nt(wvt_ref[...], hb).astype(BF16)
    dtf_ref[...] = _dot(hb, wdtf_ref[...])
    dtft_ref[...] = _dot_nt(wdtft_ref[...], hb)


def _in_proj(x2, norm_w, wz, wxbc, wq, wk, wvt, wdtf, wdtft):
    m = x2.shape[0]
    tm = ROW_TILE
    row = lambda width: pl.BlockSpec((tm, width), lambda i: (i, 0))
    out_shapes = (
        jax.ShapeDtypeStruct((m, SSD_WIDTH), BF16),
        jax.ShapeDtypeStruct((m, CONV_CH), BF16),
        jax.ShapeDtypeStruct((m, ATT_WIDTH), BF16),
        jax.ShapeDtypeStruct((m, ATT_WIDTH), BF16),
        jax.ShapeDtypeStruct((ATT_WIDTH, m), BF16),
        jax.ShapeDtypeStruct((m, LANES), F32),
        jax.ShapeDtypeStruct((2 * SSD_HEADS, m), F32),
    )
    return pl.pallas_call(
        _inproj_kernel,
        grid=(m // tm,),
        in_specs=[row(D_MODEL), _const_spec(norm_w.shape), _const_spec(wz.shape), _const_spec(wxbc.shape),
                  _const_spec(wq.shape), _const_spec(wk.shape), _const_spec(wvt.shape),
                  _const_spec(wdtf.shape), _const_spec(wdtft.shape)],
        out_specs=(row(SSD_WIDTH), row(CONV_CH), row(ATT_WIDTH), row(ATT_WIDTH),
                   pl.BlockSpec((ATT_WIDTH, tm), lambda i: (0, i)), row(LANES),
                   pl.BlockSpec((2 * SSD_HEADS, tm), lambda i: (0, i))),
        out_shape=out_shapes,
        compiler_params=pltpu.CompilerParams(dimension_semantics=("arbitrary",),
                                             vmem_limit_bytes=VMEM_LIMIT_BYTES),
        name="in_proj",
    )(x2, norm_w, wz, wxbc, wq, wk, wvt, wdtf, wdtft)


def _ssd_kernel(xbc_ref, z_ref, dtf_ref, dtft_ref, convw_ref, convb_ref, cbias_ref, calog_ref,
                rbias_ref, ralog_ref, dskip_ref, nw_ref, tri3_ref, trit3_ref, pq_ref, pk_ref,
                oq_ref, ok_ref, y_ref, cq_ref, ck_ref, tail_ref, st_ref, carry_ref):
    L = CHUNK

    @pl.when(pl.program_id(1) == 0)
    def _():
        tail_ref[...] = jnp.zeros_like(tail_ref)
        st_ref[...] = jnp.zeros_like(st_ref)
        carry_ref[...] = jnp.zeros_like(carry_ref)

    u = xbc_ref[...].astype(F32)
    ext = jnp.concatenate([tail_ref[...], u], axis=0)
    w = convw_ref[...]
    acc = convb_ref[...] + u * w[CONV_WIDTH - 1:CONV_WIDTH, :]
    for back in range(1, CONV_WIDTH):
        shifted = pltpu.roll(ext, back, axis=0)[SUBLANES:SUBLANES + L, :]
        acc = acc + shifted * w[CONV_WIDTH - 1 - back:CONV_WIDTH - back, :]
    tail_ref[...] = u[L - SUBLANES:L, :]
    xc = _silu(acc)
    xs = xc[:, :SSD_WIDTH]
    xsb = xs.astype(BF16)
    bmat = xc[:, SSD_WIDTH:SSD_WIDTH + SSD_GROUPS * SSD_STATE]
    cmat = xc[:, SSD_WIDTH + SSD_GROUPS * SSD_STATE:]

    lane = lax.broadcasted_iota(jnp.int32, (L, LANES), 1)
    raw = dtf_ref[...] + cbias_ref[...]
    a_col = -jnp.exp(calog_ref[...]) * _softplus(raw)
    vcol = jnp.where(lane < SSD_HEADS, a_col, jnp.where(lane < 2 * SSD_HEADS, -_softplus(-raw), 0.0))
    cum = _dot(tri3_ref[...], jnp.concatenate(_split3(vcol), axis=0))

    row = lax.broadcasted_iota(jnp.int32, (2 * SSD_HEADS, L), 0)
    dt_t = _softplus(dtft_ref[...] + rbias_ref[...])
    a_t = jnp.where(row < SSD_HEADS, -jnp.exp(ralog_ref[...]) * dt_t, 0.0)
    acum_t = _dot(jnp.concatenate(_split3(a_t), axis=1), trit3_ref[...])
    last_t = acum_t[:, L - 1:L]
    w_t = jnp.exp(last_t - acum_t) * dt_t

    ri = lax.broadcasted_iota(jnp.int32, (L, L), 0)
    ci = lax.broadcasted_iota(jnp.int32, (L, L), 1)
    causal = ri >= ci

    cb, b_t, c_f = [], [], []
    for g in range(SSD_GROUPS):
        bg = bmat[:, g * SSD_STATE:(g + 1) * SSD_STATE]
        cg = cmat[:, g * SSD_STATE:(g + 1) * SSD_STATE]
        cb.append(_dot_nt(cg.astype(BF16), bg.astype(BF16)))
        b_t.append(bg.T)
        c_f.append(cg)

    g_mats, ce_mats, btw_mats, cdecay = [], [], [], []
    for h in range(SSD_HEADS):
        g = h // (SSD_HEADS // SSD_GROUPS)
        acol = jnp.broadcast_to(cum[:, h:h + 1], (L, L))
        arow = acum_t[h:h + 1, :]
        decay = jnp.where(causal, jnp.exp(acol - arow), 0.0)
        g_mats.append((cb[g] * decay * dt_t[h:h + 1, :]).astype(BF16))
        ce_mats.append((c_f[g] * jnp.exp(acol)).astype(BF16))
        btw_mats.append((b_t[g] * w_t[h:h + 1, :]).astype(BF16))
        cdecay.append(jnp.exp(jnp.broadcast_to(last_t[h:h + 1, :], (L, LANES))))

    lo_half = lane < SSD_HEAD_DIM
    st_mask = jnp.concatenate([lo_half, jnp.logical_not(lo_half)], axis=0)
    ys = []
    for i in range(N_PAIRS):
        h0, h1 = 2 * i, 2 * i + 1
        xp = xsb[:, i * LANES:(i + 1) * LANES]
        zero = jnp.zeros_like(xp)
        st = st_ref[i]
        rhs = jnp.concatenate([jnp.where(lo_half, xp, zero), jnp.where(lo_half, zero, xp),
                               st.astype(BF16)], axis=0)
        lhs = jnp.concatenate([g_mats[h0], g_mats[h1], ce_mats[h0], ce_mats[h1]], axis=1)
        ys.append(_dot(lhs, rhs))
        snew = _dot(jnp.concatenate([btw_mats[h0], btw_mats[h1]], axis=0), xp)
        cd = jnp.concatenate([cdecay[h0], cdecay[h1]], axis=0)
        st_ref[i] = jnp.where(st_mask, st * cd + snew, 0.0)

    y = jnp.concatenate(ys, axis=1) + dskip_ref[...] * xs
    y = y * _silu(z_ref[...].astype(F32))
    gw = SSD_WIDTH // SSD_GROUPS
    normed = []
    for g in range(SSD_GROUPS):
        yg = y[:, g * gw:(g + 1) * gw]
        normed.append(yg * lax.rsqrt(jnp.mean(yg * yg, axis=-1, keepdims=True) + EPS))
    y_ref[...] = (jnp.concatenate(normed, axis=1) * nw_ref[...]).astype(BF16)

    c_run = cum + carry_ref[0:1, :]
    in_f = jnp.logical_and(lane[0:1, :] >= SSD_HEADS, lane[0:1, :] < 2 * SSD_HEADS)
    carry_ref[0:1, :] = jnp.where(in_f, c_run[L - 1:L, :], 0.0)
    c3 = jnp.concatenate(_split3(c_run * LOG2E), axis=1)
    cq_ref[...] = (_dot(c3, pq_ref[...]) + oq_ref[...]).astype(BF16)
    ck_ref[...] = (_dot(c3, pk_ref[...]) + ok_ref[...]).astype(BF16)


def _ssd(xbc, z, dtf, dtft, consts, bsz, seq):
    m = xbc.shape[0]
    nc = seq // CHUNK
    row = lambda width: pl.BlockSpec((CHUNK, width), lambda b, c: (b * nc + c, 0))
    in_specs = [row(CONV_CH), row(SSD_WIDTH), row(LANES),
                pl.BlockSpec((2 * SSD_HEADS, CHUNK), lambda b, c: (0, b * nc + c))]
    in_specs += [_const_spec(a.shape) for a in consts]
    return pl.pallas_call(
        _ssd_kernel,
        grid=(bsz, nc),
        in_specs=in_specs,
        out_specs=(row(SSD_WIDTH), row(LANES), row(LANES)),
        out_shape=(jax.ShapeDtypeStruct((m, SSD_WIDTH), BF16),
                   jax.ShapeDtypeStruct((m, LANES), BF16),
                   jax.ShapeDtypeStruct((m, LANES), BF16)),
        scratch_shapes=[pltpu.VMEM((SUBLANES, CONV_CH), F32),
                        pltpu.VMEM((N_PAIRS, 2 * CHUNK, LANES), F32),
                        pltpu.VMEM((SUBLANES, LANES), F32)],
        compiler_params=pltpu.CompilerParams(dimension_semantics=("arbitrary", "arbitrary"),
                                             vmem_limit_bytes=VMEM_LIMIT_BYTES),
        name="ssd",
    )(xbc, z, dtf, dtft, *consts)


def _attn_kernel(q_ref, k_ref, vt_ref, cq_ref, ck_ref, o_ref, qx_ref, m_ref, acc_ref):
    t = ATT_TILE
    pair = pl.program_id(1)
    qi = pl.program_id(2)

    lane = lax.broadcasted_iota(jnp.int32, (t, LANES), 1)
    lo_half = lane < ATT_HEAD_DIM
    qs = q_ref[0]
    cqv = cq_ref[0]
    zero = jnp.zeros_like(qs)
    for hh in range(HEADS_PER_LANE_TILE):
        head = pair * HEADS_PER_LANE_TILE + hh
        qm = jnp.where(lo_half, qs, zero) if hh == 0 else jnp.where(lo_half, zero, qs)
        cm = jnp.where((lane // BIAS_LANES_PER_HEAD) == head, cqv, zero)
        qx_ref[hh * t:(hh + 1) * t, :] = jnp.concatenate([qm, cm], axis=1)

    cc = ATT_QUERY_CHUNK

    def ones_rows(nk):
        return jnp.where(lax.broadcasted_iota(jnp.int32, (ATT_DENOM_ROWS, nk), 0) == 0, 1.0, 0.0).astype(BF16)

    def scores(j):
        start = pl.multiple_of(j * t, t)
        kx = jnp.concatenate([k_ref[0, pl.ds(start, t), :], ck_ref[0, pl.ds(start, t), :]], axis=1)
        return _dot_nt(kx, qx_ref[...])

    def softmax_pv(s_all, j, first):
        start = pl.multiple_of(j * t, t)
        for c in range(HEADS_PER_LANE_TILE * t // cc):
            cols = slice(c * cc, (c + 1) * cc)
            hh = (c * cc) // t
            off = (c * cc) % t
            nk = off + cc if first else t
            vh = jnp.concatenate([vt_ref[hh * ATT_HEAD_DIM:(hh + 1) * ATT_HEAD_DIM, pl.ds(start, nk)],
                                  ones_rows(nk)], axis=0)
            s = s_all[:nk, cols]
            if first:
                key_i = lax.broadcasted_iota(jnp.int32, (nk, cc), 0)
                qry_i = lax.broadcasted_iota(jnp.int32, (nk, cc), 1)
                s = jnp.where(key_i <= qry_i + off, s, -jnp.inf)
                m_new = jnp.max(s, axis=0, keepdims=True)
                p = jnp.exp2(s - m_new)
                acc_ref[:, cols] = _dot(vh, p.astype(BF16))
            else:
                m_old = m_ref[0:1, cols]
                m_new = jnp.maximum(m_old, jnp.max(s, axis=0, keepdims=True))
                alpha = jnp.exp2(m_old - m_new)
                p = jnp.exp2(s - m_new)
                acc_ref[:, cols] = alpha * acc_ref[:, cols] + _dot(vh, p.astype(BF16))
            m_ref[0:1, cols] = m_new

    softmax_pv(scores(qi), qi, True)

    def body(j, carry):
        softmax_pv(scores(j), j, False)
        return carry

    lax.fori_loop(0, qi, body, 0)

    d = ATT_HEAD_DIM
    o_t = jnp.concatenate([acc_ref[0:d, hh * t:(hh + 1) * t] / acc_ref[d:d + 1, hh * t:(hh + 1) * t]
                           for hh in range(HEADS_PER_LANE_TILE)], axis=0)
    o_ref[0] = o_t.T.astype(BF16)


def _attention(q3, k3, vt, cq3, ck3):
    bsz, seq, _ = q3.shape
    t = ATT_TILE
    cols = HEADS_PER_LANE_TILE * t
    return pl.pallas_call(
        _attn_kernel,
        grid=(bsz, N_PAIRS, seq // t),
        in_specs=[pl.BlockSpec((1, t, LANES), lambda b, p, i: (b, i, p)),
                  pl.BlockSpec((1, seq, LANES), lambda b, p, i: (b, 0, p)),
                  pl.BlockSpec((LANES, seq), lambda b, p, i: (p, b)),
                  pl.BlockSpec((1, t, LANES), lambda b, p, i: (b, i, 0)),
                  pl.BlockSpec((1, seq, LANES), lambda b, p, i: (b, 0, 0))],
        out_specs=pl.BlockSpec((1, t, LANES), lambda b, p, i: (b, i, p)),
        out_shape=jax.ShapeDtypeStruct((bsz, seq, ATT_WIDTH), BF16),
        scratch_shapes=[pltpu.VMEM((cols, 2 * LANES), BF16),
                        pltpu.VMEM((SUBLANES, cols), F32),
                        pltpu.VMEM((ATT_HEAD_DIM + ATT_DENOM_ROWS, cols), F32)],
        compiler_params=pltpu.CompilerParams(dimension_semantics=("arbitrary",) * 3,
                                             vmem_limit_bytes=VMEM_LIMIT_BYTES),
        name="fox_attn",
    )(q3, k3, vt, cq3, ck3)


def _out_mlp_kernel(x_ref, ys_ref, ya_ref, wos_ref, woa_ref, nmw_ref, wup_ref, wdn_ref, nfw_ref, o_ref):
    h1 = x_ref[...] + _dot(ys_ref[...], wos_ref[...]) + _dot(ya_ref[...], woa_ref[...])
    ms = jnp.mean(h1 * h1, axis=-1, keepdims=True)
    hb = ((h1 * lax.rsqrt(ms + EPS)) * nmw_ref[...]).astype(BF16)
    o_ref[...] = h1
    for c in range(D_FF // D_MODEL):
        u = _dot(hb, wup_ref[:, c * D_MODEL:(c + 1) * D_MODEL])
        u = jnp.square(jnp.maximum(u, 0.0))
        o_ref[...] += _dot(u.astype(BF16), wdn_ref[c * D_MODEL:(c + 1) * D_MODEL, :])
    acc = o_ref[...]
    ms2 = jnp.mean(acc * acc, axis=-1, keepdims=True)
    o_ref[...] = (acc * lax.rsqrt(ms2 + EPS)) * nfw_ref[...]


def _out_mlp(x2, ys, ya, wos, woa, nmw, wup, wdn, nfw):
    m = x2.shape[0]
    tm = ROW_TILE
    row = lambda: pl.BlockSpec((tm, D_MODEL), lambda i: (i, 0))
    consts = (wos, woa, nmw, wup, wdn, nfw)
    return pl.pallas_call(
        _out_mlp_kernel,
        grid=(m // tm,),
        in_specs=[row(), row(), row()] + [_const_spec(a.shape) for a in consts],
        out_specs=row(),
        out_shape=jax.ShapeDtypeStruct((m, D_MODEL), F32),
        compiler_params=pltpu.CompilerParams(dimension_semantics=("arbitrary",),
                                             vmem_limit_bytes=VMEM_LIMIT_BYTES),
        name="out_mlp",
    )(x2, ys, ya, *consts)


def _ssd_constants():
    t = np.arange(CHUNK)
    tri = (t[None, :] <= t[:, None]).astype(np.float32)
    tri3 = np.concatenate([tri, tri, tri], axis=1)
    trit3 = np.concatenate([tri.T, tri.T, tri.T], axis=0)
    pq = np.zeros((3 * LANES, LANES), np.float32)
    pk = np.zeros((3 * LANES, LANES), np.float32)
    oq = np.zeros((1, LANES), np.float32)
    ok = np.zeros((1, LANES), np.float32)
    for h in range(ATT_HEADS):
        base = h * BIAS_LANES_PER_HEAD
        for j in range(3):
            src = j * LANES + SSD_HEADS + h
            pq[src, base + j] = 1.0
            pk[src, base + 3 + j] = -1.0
            ok[0, base + j] = 1.0
            oq[0, base + 3 + j] = 1.0
    return (jnp.asarray(tri3, BF16), jnp.asarray(trit3, BF16), jnp.asarray(pq, BF16), jnp.asarray(pk, BF16),
            jnp.asarray(oq), jnp.asarray(ok))


def kernel(x, norm_mix_w, w_in, conv_w, conv_b, dt_bias, a_log, d_skip, ssd_norm_w, f_bias, w_out,
           norm_mlp_w, w_up, w_down, norm_final_w):
    bsz, seq, _ = x.shape
    m = bsz * seq
    assert norm_mix_w.shape[0] == 1, "single layer"
    assert m % ROW_TILE == 0 and seq % ATT_TILE == 0 and seq % CHUNK == 0
    x2 = x.reshape(m, D_MODEL)

    w = w_in[0]
    o_z, o_xbc = SSD_WIDTH, SSD_WIDTH + CONV_CH
    o_dt = o_xbc + SSD_HEADS
    o_q, o_k, o_v = o_dt + ATT_WIDTH, o_dt + 2 * ATT_WIDTH, o_dt + 3 * ATT_WIDTH
    wz = w[:, :o_z].astype(BF16)
    wxbc = w[:, o_z:o_xbc].astype(BF16)
    wq = w[:, o_dt:o_q].astype(BF16)
    wk = w[:, o_q:o_k].astype(BF16)
    wvt = w[:, o_k:o_v].T.astype(BF16)
    w_dtf = jnp.concatenate([w[:, o_xbc:o_dt], w[:, o_v:]], axis=1)
    wdtf = jnp.pad(w_dtf, ((0, 0), (0, LANES - 2 * SSD_HEADS))).astype(BF16)
    wdtft = w_dtf.T.astype(BF16)

    z, xbc, q, k, vt, dtf, dtft = _in_proj(x2, norm_mix_w[0][None, :], wz, wxbc, wq, wk, wvt, wdtf, wdtft)

    pad_lanes = lambda a: jnp.pad(a, (0, LANES - a.shape[0]))[None, :]
    col_bias = pad_lanes(jnp.concatenate([dt_bias[0], f_bias[0]]))
    col_alog = pad_lanes(a_log[0])
    row_bias = jnp.broadcast_to(jnp.concatenate([dt_bias[0], f_bias[0]])[:, None], (2 * SSD_HEADS, CHUNK))
    row_alog = jnp.broadcast_to(jnp.pad(a_log[0], (0, SSD_HEADS))[:, None], (2 * SSD_HEADS, CHUNK))
    dskip = jnp.repeat(d_skip[0], SSD_HEAD_DIM)[None, :]
    consts = (conv_w[0], conv_b[0][None, :], col_bias, col_alog, row_bias, row_alog, dskip,
              ssd_norm_w[0][None, :]) + _ssd_constants()
    y_ssd, cq, ck = _ssd(xbc, z, dtf, dtft, consts, bsz, seq)

    to3 = lambda a: a.reshape(bsz, seq, a.shape[-1])
    y_att = _attention(to3(q), to3(k), vt, to3(cq), to3(ck)).reshape(m, ATT_WIDTH)

    wo = w_out[0].astype(BF16)
    out = _out_mlp(x2, y_ssd, y_att, wo[:SSD_WIDTH], wo[SSD_WIDTH:], norm_mlp_w[0][None, :],
                   w_up[0].astype(BF16), w_down[0].astype(BF16), norm_final_w[None, :])
    return out.reshape(bsz, seq, D_MODEL)
```

```python
import functools

import numpy as np
import jax
import jax.numpy as jnp
from jax import lax
from jax.experimental import pallas as pl
from jax.experimental.pallas import tpu as pltpu

D_MODEL = 1024
SSD_HEADS = 16
SSD_HEAD_DIM = 64
SSD_WIDTH = SSD_HEADS * SSD_HEAD_DIM
SSD_GROUPS = 2
SSD_STATE = 128
CONV_WIDTH = 4
CHUNK = 128
CONV_CH = SSD_WIDTH + 2 * SSD_GROUPS * SSD_STATE
ATT_HEADS = 16
ATT_HEAD_DIM = 64
ATT_WIDTH = ATT_HEADS * ATT_HEAD_DIM
D_FF = 4 * D_MODEL
EPS = 1e-5

LANES = 128
SUBLANES = 8
HEADS_PER_LANE_TILE = LANES // SSD_HEAD_DIM
N_PAIRS = SSD_HEADS // HEADS_PER_LANE_TILE
BIAS_LANES_PER_HEAD = LANES // ATT_HEADS
VMEM_LIMIT_BYTES = 56 * 1024 * 1024

ROW_TILE = 512
ATT_TILE = 1024
ATT_QUERY_CHUNK = 256
ATT_DENOM_ROWS = 16
LOG2E = 1.4426950408889634
QK_SCALE = LOG2E * ATT_HEAD_DIM ** -0.5

F32 = jnp.float32
BF16 = jnp.bfloat16

_NT = (((1,), (1,)), ((), ()))


def _dot(a, b):
    return jnp.dot(a, b, preferred_element_type=F32)


def _dot_nt(a, b):
    return lax.dot_general(a, b, _NT, preferred_element_type=F32)


def _softplus(x):
    return jnp.maximum(x, 0.0) + jnp.log1p(jnp.exp(-jnp.abs(x)))


def _silu(x):
    return x / (1.0 + jnp.exp(-x))


def _split3(v):
    hi = v.astype(BF16)
    r1 = v - hi.astype(F32)
    mid = r1.astype(BF16)
    lo = (r1 - mid.astype(F32)).astype(BF16)
    return hi, mid, lo


def _const_spec(shape):
    zeros = (0,) * len(shape)
    return pl.BlockSpec(shape, lambda *_: zeros, pipeline_mode=pl.Buffered(1))


def _inproj_kernel(x_ref, nw_ref, wz_ref, wxbc_ref, wq_ref, wk_ref, wvt_ref, wdtf_ref, wdtft_ref,
                   z_ref, xbc_ref, q_ref, k_ref, vt_ref, dtf_ref, dtft_ref):
    x = x_ref[...]
    ms = jnp.mean(x * x, axis=-1, keepdims=True)
    hb = ((x * lax.rsqrt(ms + EPS)) * nw_ref[...]).astype(BF16)
    z_ref[...] = _dot(hb, wz_ref[...]).astype(BF16)
    xbc_ref[...] = _dot(hb, wxbc_ref[...]).astype(BF16)
    q_ref[...] = (_dot(hb, wq_ref[...]) * QK_SCALE).astype(BF16)
    k_ref[...] = _dot(hb, wk_ref[...]).astype(BF16)
    vt_ref[...] = _dot_nt(wvt_ref[...], hb).astype(BF16)
    dtf_ref[...] = _dot(hb, wdtf_ref[...])
    dtft_ref[...] = _dot_nt(wdtft_ref[...], hb)


def _in_proj(x2, norm_w, wz, wxbc, wq, wk, wvt, wdtf, wdtft):
    m = x2.shape[0]
    tm = ROW_TILE
    row = lambda width: pl.BlockSpec((tm, width), lambda i: (i, 0))
    out_shapes = (
        jax.ShapeDtypeStruct((m, SSD_WIDTH), BF16),
        jax.ShapeDtypeStruct((m, CONV_CH), BF16),
        jax.ShapeDtypeStruct((m, ATT_WIDTH), BF16),
        jax.ShapeDtypeStruct((m, ATT_WIDTH), BF16),
        jax.ShapeDtypeStruct((ATT_WIDTH, m), BF16),
        jax.ShapeDtypeStruct((m, LANES), F32),
        jax.ShapeDtypeStruct((2 * SSD_HEADS, m), F32),
    )
    return pl.pallas_call(
        _inproj_kernel,
        grid=(m // tm,),
        in_specs=[row(D_MODEL), _const_spec(norm_w.shape), _const_spec(wz.shape), _const_spec(wxbc.shape),
                  _const_spec(wq.shape), _const_spec(wk.shape), _const_spec(wvt.shape),
                  _const_spec(wdtf.shape), _const_spec(wdtft.shape)],
        out_specs=(row(SSD_WIDTH), row(CONV_CH), row(ATT_WIDTH), row(ATT_WIDTH),
                   pl.BlockSpec((ATT_WIDTH, tm), lambda i: (0, i)), row(LANES),
                   pl.BlockSpec((2 * SSD_HEADS, tm), lambda i: (0, i))),
        out_shape=out_shapes,
        compiler_params=pltpu.CompilerParams(dimension_semantics=("arbitrary",),
                                             vmem_limit_bytes=VMEM_LIMIT_BYTES),
        name="in_proj",
    )(x2, norm_w, wz, wxbc, wq, wk, wvt, wdtf, wdtft)


def _ssd_kernel(xbc_ref, z_ref, dtf_ref, dtft_ref, convw_ref, convb_ref, cbias_ref, calog_ref,
                rbias_ref, ralog_ref, dskip_ref, nw_ref, tri3_ref, trit3_ref, pq_ref, pk_ref,
                oq_ref, ok_ref, y_ref, cq_ref, ck_ref, tail_ref, st_ref, carry_ref):
    L = CHUNK

    @pl.when(pl.program_id(1) == 0)
    def _():
        tail_ref[...] = jnp.zeros_like(tail_ref)
        st_ref[...] = jnp.zeros_like(st_ref)
        carry_ref[...] = jnp.zeros_like(carry_ref)

    u = xbc_ref[...].astype(F32)
    ext = jnp.concatenate([tail_ref[...], u], axis=0)
    w = convw_ref[...]
    acc = convb_ref[...] + u * w[CONV_WIDTH - 1:CONV_WIDTH, :]
    for back in range(1, CONV_WIDTH):
        shifted = pltpu.roll(ext, back, axis=0)[SUBLANES:SUBLANES + L, :]
        acc = acc + shifted * w[CONV_WIDTH - 1 - back:CONV_WIDTH - back, :]
    tail_ref[...] = u[L - SUBLANES:L, :]
    xc = _silu(acc)
    xs = xc[:, :SSD_WIDTH]
    xsb = xs.astype(BF16)
    bmat = xc[:, SSD_WIDTH:SSD_WIDTH + SSD_GROUPS * SSD_STATE]
    cmat = xc[:, SSD_WIDTH + SSD_GROUPS * SSD_STATE:]

    lane = lax.broadcasted_iota(jnp.int32, (L, LANES), 1)
    raw = dtf_ref[...] + cbias_ref[...]
    sp = _softplus(raw)
    a_col = -jnp.exp(calog_ref[...]) * sp
    vcol = jnp.where(lane < SSD_HEADS, a_col, jnp.where(lane < 2 * SSD_HEADS, raw - sp, 0.0))
    cum = _dot(tri3_ref[...], jnp.concatenate(_split3(vcol), axis=0))
    acol2 = cum * LOG2E

    row = lax.broadcasted_iota(jnp.int32, (2 * SSD_HEADS, L), 0)
    dt_t = _softplus(dtft_ref[...] + rbias_ref[...])
    a_t = jnp.where(row < SSD_HEADS, -jnp.exp(ralog_ref[...]) * dt_t, 0.0)
    acum2_t = _dot(jnp.concatenate(_split3(a_t), axis=1), trit3_ref[...]) * LOG2E
    last2_t = acum2_t[:, L - 1:L]
    w_t = jnp.exp2(last2_t - acum2_t) * dt_t
    arow_t = acum2_t - jnp.log2(dt_t)

    ri = lax.broadcasted_iota(jnp.int32, (L, L), 0)
    ci = lax.broadcasted_iota(jnp.int32, (L, L), 1)
    causal = ri >= ci

    cb, b_t, c_f = [], [], []
    for g in range(SSD_GROUPS):
        bg = bmat[:, g * SSD_STATE:(g + 1) * SSD_STATE]
        cg = cmat[:, g * SSD_STATE:(g + 1) * SSD_STATE]
        cb.append(_dot_nt(cg.astype(BF16), bg.astype(BF16)))
        b_t.append(bg.T)
        c_f.append(cg)

    g_mats, ce_mats, btw_mats, cdecay = [], [], [], []
    for h in range(SSD_HEADS):
        g = h // (SSD_HEADS // SSD_GROUPS)
        acol = jnp.broadcast_to(acol2[:, h:h + 1], (L, L))
        g_mats.append((cb[g] * jnp.where(causal, jnp.exp2(acol - arow_t[h:h + 1, :]), 0.0)).astype(BF16))
        ce_mats.append((c_f[g] * jnp.exp2(acol)).astype(BF16))
        btw_mats.append((b_t[g] * w_t[h:h + 1, :]).astype(BF16))
        cdecay.append(jnp.exp2(jnp.broadcast_to(last2_t[h:h + 1, :], (L, LANES))))

    lo_half = lane < SSD_HEAD_DIM
    st_mask = jnp.concatenate([lo_half, jnp.logical_not(lo_half)], axis=0)
    ys = []
    for i in range(N_PAIRS):
        h0, h1 = 2 * i, 2 * i + 1
        xp = xsb[:, i * LANES:(i + 1) * LANES]
        zero = jnp.zeros_like(xp)
        st = st_ref[i]
        rhs = jnp.concatenate([jnp.where(lo_half, xp, zero), jnp.where(lo_half, zero, xp),
                               st.astype(BF16)], axis=0)
        lhs = jnp.concatenate([g_mats[h0], g_mats[h1], ce_mats[h0], ce_mats[h1]], axis=1)
        ys.append(_dot(lhs, rhs))
        snew = _dot(jnp.concatenate([btw_mats[h0], btw_mats[h1]], axis=0), xp)
        cd = jnp.concatenate([cdecay[h0], cdecay[h1]], axis=0)
        st_ref[i] = jnp.where(st_mask, st * cd + snew, 0.0)

    y = jnp.concatenate(ys, axis=1) + dskip_ref[...] * xs
    y = y * _silu(z_ref[...].astype(F32))
    gw = SSD_WIDTH // SSD_GROUPS
    normed = []
    for g in range(SSD_GROUPS):
        yg = y[:, g * gw:(g + 1) * gw]
        normed.append(yg * lax.rsqrt(jnp.mean(yg * yg, axis=-1, keepdims=True) + EPS))
    y_ref[...] = (jnp.concatenate(normed, axis=1) * nw_ref[...]).astype(BF16)

    c_run = cum + carry_ref[0:1, :]
    in_f = jnp.logical_and(lane[0:1, :] >= SSD_HEADS, lane[0:1, :] < 2 * SSD_HEADS)
    carry_ref[0:1, :] = jnp.where(in_f, c_run[L - 1:L, :], 0.0)
    c3 = jnp.concatenate(_split3(c_run * LOG2E), axis=1)
    cq_ref[...] = (_dot(c3, pq_ref[...]) + oq_ref[...]).astype(BF16)
    ck_ref[...] = (_dot(c3, pk_ref[...]) + ok_ref[...]).astype(BF16)


def _ssd(xbc, z, dtf, dtft, consts, bsz, seq):
    m = xbc.shape[0]
    nc = seq // CHUNK
    row = lambda width: pl.BlockSpec((CHUNK, width), lambda b, c: (b * nc + c, 0))
    in_specs = [row(CONV_CH), row(SSD_WIDTH), row(LANES),
                pl.BlockSpec((2 * SSD_HEADS, CHUNK), lambda b, c: (0, b * nc + c))]
    in_specs += [_const_spec(a.shape) for a in consts]
    return pl.pallas_call(
        _ssd_kernel,
        grid=(bsz, nc),
        in_specs=in_specs,
        out_specs=(row(SSD_WIDTH), row(LANES), row(LANES)),
        out_shape=(jax.ShapeDtypeStruct((m, SSD_WIDTH), BF16),
                   jax.ShapeDtypeStruct((m, LANES), BF16),
                   jax.ShapeDtypeStruct((m, LANES), BF16)),
        scratch_shapes=[pltpu.VMEM((SUBLANES, CONV_CH), F32),
                        pltpu.VMEM((N_PAIRS, 2 * CHUNK, LANES), F32),
                        pltpu.VMEM((SUBLANES, LANES), F32)],
        compiler_params=pltpu.CompilerParams(dimension_semantics=("arbitrary", "arbitrary"),
                                             vmem_limit_bytes=VMEM_LIMIT_BYTES),
        name="ssd",
    )(xbc, z, dtf, dtft, *consts)


def _attn_kernel(q_ref, k_ref, vt_ref, cq_ref, ck_ref, o_ref, qx_ref, m_ref, acc_ref):
    t = ATT_TILE
    n_tiles = q_ref.shape[1] // t
    pair = pl.program_id(1)
    cc = ATT_QUERY_CHUNK
    d = ATT_HEAD_DIM

    lane = lax.broadcasted_iota(jnp.int32, (t, LANES), 1)
    lo_half = lane < d
    for qi in range(n_tiles):
        qs = q_ref[0, qi * t:(qi + 1) * t, :]
        cqv = cq_ref[0, qi * t:(qi + 1) * t, :]
        zero = jnp.zeros_like(qs)
        for hh in range(HEADS_PER_LANE_TILE):
            head = pair * HEADS_PER_LANE_TILE + hh
            qm = jnp.where(lo_half, qs, zero) if hh == 0 else jnp.where(lo_half, zero, qs)
            cm = jnp.where((lane // BIAS_LANES_PER_HEAD) == head, cqv, zero)
            qx_ref[qi, hh * t:(hh + 1) * t, :] = jnp.concatenate([qm, cm], axis=1)

    def ones_rows(nk):
        return jnp.where(lax.broadcasted_iota(jnp.int32, (ATT_DENOM_ROWS, nk), 0) == 0, 1.0, 0.0).astype(BF16)

    def scores(qi, j):
        kx = jnp.concatenate([k_ref[0, j * t:(j + 1) * t, :], ck_ref[0, j * t:(j + 1) * t, :]], axis=1)
        return _dot_nt(kx, qx_ref[qi])

    def softmax_pv(s_all, qi, j):
        first = j == qi
        for c in range(HEADS_PER_LANE_TILE * t // cc):
            cols = slice(c * cc, (c + 1) * cc)
            hh = (c * cc) // t
            off = (c * cc) % t
            nk = off + cc if first else t
            vh = jnp.concatenate([vt_ref[hh * d:(hh + 1) * d, j * t:j * t + nk], ones_rows(nk)], axis=0)
            s = s_all[:nk, cols]
            if first:
                key_i = lax.broadcasted_iota(jnp.int32, (nk, cc), 0)
                qry_i = lax.broadcasted_iota(jnp.int32, (nk, cc), 1)
                s = jnp.where(key_i <= qry_i + off, s, -jnp.inf)
                m_new = jnp.max(s, axis=0, keepdims=True)
                p = jnp.exp2(s - m_new)
                acc_ref[qi, :, cols] = _dot(vh, p.astype(BF16))
            else:
                m_old = m_ref[qi, 0:1, cols]
                m_new = jnp.maximum(m_old, jnp.max(s, axis=0, keepdims=True))
                alpha = jnp.exp2(m_old - m_new)
                p = jnp.exp2(s - m_new)
                acc_ref[qi, :, cols] = alpha * acc_ref[qi, :, cols] + _dot(vh, p.astype(BF16))
            m_ref[qi, 0:1, cols] = m_new

    for qi in range(n_tiles):
        softmax_pv(scores(qi, qi), qi, qi)
    for qi in range(n_tiles):
        for j in range(qi):
            softmax_pv(scores(qi, j), qi, j)
        o_t = jnp.concatenate([acc_ref[qi, 0:d, hh * t:(hh + 1) * t] / acc_ref[qi, d:d + 1, hh * t:(hh + 1) * t]
                               for hh in range(HEADS_PER_LANE_TILE)], axis=0)
        o_ref[0, qi * t:(qi + 1) * t, :] = o_t.T.astype(BF16)


def _attention(q3, k3, vt, cq3, ck3):
    bsz, seq, _ = q3.shape
    t = ATT_TILE
    n_tiles = seq // t
    cols = HEADS_PER_LANE_TILE * t
    seq_block = lambda lane_block: pl.BlockSpec((1, seq, LANES), lane_block)
    return pl.pallas_call(
        _attn_kernel,
        grid=(bsz, N_PAIRS),
        in_specs=[seq_block(lambda b, p: (b, 0, p)),
                  seq_block(lambda b, p: (b, 0, p)),
                  pl.BlockSpec((LANES, seq), lambda b, p: (p, b)),
                  seq_block(lambda b, p: (b, 0, 0)),
                  seq_block(lambda b, p: (b, 0, 0))],
        out_specs=seq_block(lambda b, p: (b, 0, p)),
        out_shape=jax.ShapeDtypeStruct((bsz, seq, ATT_WIDTH), BF16),
        scratch_shapes=[pltpu.VMEM((n_tiles, cols, 2 * LANES), BF16),
                        pltpu.VMEM((n_tiles, SUBLANES, cols), F32),
                        pltpu.VMEM((n_tiles, ATT_HEAD_DIM + ATT_DENOM_ROWS, cols), F32)],
        compiler_params=pltpu.CompilerParams(dimension_semantics=("arbitrary",) * 2,
                                             vmem_limit_bytes=VMEM_LIMIT_BYTES),
        name="fox_attn",
    )(q3, k3, vt, cq3, ck3)


def _out_mlp_kernel(x_ref, ys_ref, ya_ref, wos_ref, woa_ref, nmw_ref, wup_ref, wdn_ref, nfw_ref, o_ref):
    h1 = x_ref[...] + _dot(ys_ref[...], wos_ref[...]) + _dot(ya_ref[...], woa_ref[...])
    ms = jnp.mean(h1 * h1, axis=-1, keepdims=True)
    hb = ((h1 * lax.rsqrt(ms + EPS)) * nmw_ref[...]).astype(BF16)
    o_ref[...] = h1
    for c in range(D_FF // D_MODEL):
        u = _dot(hb, wup_ref[:, c * D_MODEL:(c + 1) * D_MODEL])
        u = jnp.square(jnp.maximum(u, 0.0))
        o_ref[...] += _dot(u.astype(BF16), wdn_ref[c * D_MODEL:(c + 1) * D_MODEL, :])
    acc = o_ref[...]
    ms2 = jnp.mean(acc * acc, axis=-1, keepdims=True)
    o_ref[...] = (acc * lax.rsqrt(ms2 + EPS)) * nfw_ref[...]


def _out_mlp(x2, ys, ya, wos, woa, nmw, wup, wdn, nfw):
    m = x2.shape[0]
    tm = ROW_TILE
    row = lambda: pl.BlockSpec((tm, D_MODEL), lambda i: (i, 0))
    consts = (wos, woa, nmw, wup, wdn, nfw)
    return pl.pallas_call(
        _out_mlp_kernel,
        grid=(m // tm,),
        in_specs=[row(), row(), row()] + [_const_spec(a.shape) for a in consts],
        out_specs=row(),
        out_shape=jax.ShapeDtypeStruct((m, D_MODEL), F32),
        compiler_params=pltpu.CompilerParams(dimension_semantics=("arbitrary",),
                                             vmem_limit_bytes=VMEM_LIMIT_BYTES),
        name="out_mlp",
    )(x2, ys, ya, *consts)


def _ssd_constants():
    t = np.arange(CHUNK)
    tri = (t[None, :] <= t[:, None]).astype(np.float32)
    tri3 = np.concatenate([tri, tri, tri], axis=1)
    trit3 = np.concatenate([tri.T, tri.T, tri.T], axis=0)
    pq = np.zeros((3 * LANES, LANES), np.float32)
    pk = np.zeros((3 * LANES, LANES), np.float32)
    oq = np.zeros((1, LANES), np.float32)
    ok = np.zeros((1, LANES), np.float32)
    for h in range(ATT_HEADS):
        base = h * BIAS_LANES_PER_HEAD
        for j in range(3):
            src = j * LANES + SSD_HEADS + h
            pq[src, base + j] = 1.0
            pk[src, base + 3 + j] = -1.0
            ok[0, base + j] = 1.0
            oq[0, base + 3 + j] = 1.0
    return (jnp.asarray(tri3, BF16), jnp.asarray(trit3, BF16), jnp.asarray(pq, BF16), jnp.asarray(pk, BF16),
            jnp.asarray(oq), jnp.asarray(ok))


def kernel(x, norm_mix_w, w_in, conv_w, conv_b, dt_bias, a_log, d_skip, ssd_norm_w, f_bias, w_out,
           norm_mlp_w, w_up, w_down, norm_final_w):
    bsz, seq, _ = x.shape
    m = bsz * seq
    assert norm_mix_w.shape[0] == 1, "single layer"
    assert m % ROW_TILE == 0 and seq % ATT_TILE == 0 and seq % CHUNK == 0
    x2 = x.reshape(m, D_MODEL)

    w = w_in[0]
    o_z, o_xbc = SSD_WIDTH, SSD_WIDTH + CONV_CH
    o_dt = o_xbc + SSD_HEADS
    o_q, o_k, o_v = o_dt + ATT_WIDTH, o_dt + 2 * ATT_WIDTH, o_dt + 3 * ATT_WIDTH
    wz = w[:, :o_z].astype(BF16)
    wxbc = w[:, o_z:o_xbc].astype(BF16)
    wq = w[:, o_dt:o_q].astype(BF16)
    wk = w[:, o_q:o_k].astype(BF16)
    wvt = w[:, o_k:o_v].T.astype(BF16)
    w_dtf = jnp.concatenate([w[:, o_xbc:o_dt], w[:, o_v:]], axis=1)
    wdtf = jnp.pad(w_dtf, ((0, 0), (0, LANES - 2 * SSD_HEADS))).astype(BF16)
    wdtft = w_dtf.T.astype(BF16)

    z, xbc, q, k, vt, dtf, dtft = _in_proj(x2, norm_mix_w[0][None, :], wz, wxbc, wq, wk, wvt, wdtf, wdtft)

    pad_lanes = lambda a: jnp.pad(a, (0, LANES - a.shape[0]))[None, :]
    col_bias = pad_lanes(jnp.concatenate([dt_bias[0], f_bias[0]]))
    col_alog = pad_lanes(a_log[0])
    row_bias = jnp.broadcast_to(jnp.concatenate([dt_bias[0], f_bias[0]])[:, None], (2 * SSD_HEADS, CHUNK))
    row_alog = jnp.broadcast_to(jnp.pad(a_log[0], (0, SSD_HEADS))[:, None], (2 * SSD_HEADS, CHUNK))
    dskip = jnp.repeat(d_skip[0], SSD_HEAD_DIM)[None, :]
    consts = (conv_w[0], conv_b[0][None, :], col_bias, col_alog, row_bias, row_alog, dskip,
              ssd_norm_w[0][None, :]) + _ssd_constants()
    y_ssd, cq, ck = _ssd(xbc, z, dtf, dtft, consts, bsz, seq)

    to3 = lambda a: a.reshape(bsz, seq, a.shape[-1])
    y_att = _attention(to3(q), to3(k), vt, to3(cq), to3(ck)).reshape(m, ATT_WIDTH)

    wo = w_out[0].astype(BF16)
    out = _out_mlp(x2, y_ssd, y_att, wo[:SSD_WIDTH], wo[SSD_WIDTH:], norm_mlp_w[0][None, :],
                   w_up[0].astype(BF16), w_down[0].astype(BF16), norm_final_w[None, :])
    return out.reshape(bsz, seq, D_MODEL)
```

```python
import functools

import numpy as np
import jax
import jax.numpy as jnp
from jax import lax
from jax.experimental import pallas as pl
from jax.experimental.pallas import tpu as pltpu

D_MODEL = 1024
SSD_HEADS = 16
SSD_HEAD_DIM = 64
SSD_WIDTH = SSD_HEADS * SSD_HEAD_DIM
SSD_GROUPS = 2
SSD_STATE = 128
CONV_WIDTH = 4
CHUNK = 128
CONV_CH = SSD_WIDTH + 2 * SSD_GROUPS * SSD_STATE
ATT_HEADS = 16
ATT_HEAD_DIM = 64
ATT_WIDTH = ATT_HEADS * ATT_HEAD_DIM
D_FF = 4 * D_MODEL
EPS = 1e-5

LANES = 128
SUBLANES = 8
HEADS_PER_LANE_TILE = LANES // SSD_HEAD_DIM
N_PAIRS = SSD_HEADS // HEADS_PER_LANE_TILE
BIAS_LANES_PER_HEAD = LANES // ATT_HEADS
VMEM_LIMIT_BYTES = 56 * 1024 * 1024

ROW_TILE = 512
MLP_COL_BLOCK = 256
SSD_PIECES_PER_MLP_PIECE = 3
ATT_TILE = 1024
ATT_QUERY_CHUNK = 256
ATT_DENOM_ROWS = 16
LOG2E = 1.4426950408889634
QK_SCALE = LOG2E * ATT_HEAD_DIM ** -0.5

F32 = jnp.float32
BF16 = jnp.bfloat16

_NT = (((1,), (1,)), ((), ()))


def _dot(a, b):
    return jnp.dot(a, b, preferred_element_type=F32)


def _dot_nt(a, b):
    return lax.dot_general(a, b, _NT, preferred_element_type=F32)


def _softplus(x):
    return jnp.maximum(x, 0.0) + jnp.log1p(jnp.exp(-jnp.abs(x)))


def _silu(x):
    return x / (1.0 + jnp.exp(-x))


def _split3(v):
    hi = v.astype(BF16)
    r1 = v - hi.astype(F32)
    mid = r1.astype(BF16)
    lo = (r1 - mid.astype(F32)).astype(BF16)
    return hi, mid, lo


def _const_spec(shape):
    zeros = (0,) * len(shape)
    return pl.BlockSpec(shape, lambda *_: zeros, pipeline_mode=pl.Buffered(1))


def _inproj_kernel(x_ref, nw_ref, wz_ref, wxbc_ref, wq_ref, wk_ref, wvt_ref, wdtf_ref, wdtft_ref,
                   z_ref, xbc_ref, q_ref, k_ref, vt_ref, dtf_ref, dtft_ref):
    x = x_ref[...]
    ms = jnp.mean(x * x, axis=-1, keepdims=True)
    hb = ((x * lax.rsqrt(ms + EPS)) * nw_ref[...]).astype(BF16)
    z_ref[...] = _dot(hb, wz_ref[...]).astype(BF16)
    xbc_ref[...] = _dot(hb, wxbc_ref[...]).astype(BF16)
    q_ref[...] = (_dot(hb, wq_ref[...]) * QK_SCALE).astype(BF16)
    k_ref[...] = _dot(hb, wk_ref[...]).astype(BF16)
    vt_ref[...] = _dot_nt(wvt_ref[...], hb).astype(BF16)
    dtf_ref[...] = _dot(hb, wdtf_ref[...])
    dtft_ref[...] = _dot_nt(wdtft_ref[...], hb)


def _in_proj(x2, norm_w, wz, wxbc, wq, wk, wvt, wdtf, wdtft):
    m = x2.shape[0]
    tm = ROW_TILE
    row = lambda width: pl.BlockSpec((tm, width), lambda i: (i, 0))
    out_shapes = (
        jax.ShapeDtypeStruct((m, SSD_WIDTH), BF16),
        jax.ShapeDtypeStruct((m, CONV_CH), BF16),
        jax.ShapeDtypeStruct((m, ATT_WIDTH), BF16),
        jax.ShapeDtypeStruct((m, ATT_WIDTH), BF16),
        jax.ShapeDtypeStruct((ATT_WIDTH, m), BF16),
        jax.ShapeDtypeStruct((m, LANES), F32),
        jax.ShapeDtypeStruct((2 * SSD_HEADS, m), F32),
    )
    return pl.pallas_call(
        _inproj_kernel,
        grid=(m // tm,),
        in_specs=[row(D_MODEL), _const_spec(norm_w.shape), _const_spec(wz.shape), _const_spec(wxbc.shape),
                  _const_spec(wq.shape), _const_spec(wk.shape), _const_spec(wvt.shape),
                  _const_spec(wdtf.shape), _const_spec(wdtft.shape)],
        out_specs=(row(SSD_WIDTH), row(CONV_CH), row(ATT_WIDTH), row(ATT_WIDTH),
                   pl.BlockSpec((ATT_WIDTH, tm), lambda i: (0, i)), row(LANES),
                   pl.BlockSpec((2 * SSD_HEADS, tm), lambda i: (0, i))),
        out_shape=out_shapes,
        compiler_params=pltpu.CompilerParams(dimension_semantics=("arbitrary",),
                                             vmem_limit_bytes=VMEM_LIMIT_BYTES),
        name="in_proj",
    )(x2, norm_w, wz, wxbc, wq, wk, wvt, wdtf, wdtft)


def _ssd_chunk(xbc, z, dtf, dtft, p, tail_ref, st_ref, y_out):
    L = CHUNK

    u = xbc.astype(F32)
    ext = jnp.concatenate([tail_ref[...], u], axis=0)
    w = p["conv_w"][...]
    acc = p["conv_b"][...] + u * w[CONV_WIDTH - 1:CONV_WIDTH, :]
    for back in range(1, CONV_WIDTH):
        shifted = pltpu.roll(ext, back, axis=0)[SUBLANES:SUBLANES + L, :]
        acc = acc + shifted * w[CONV_WIDTH - 1 - back:CONV_WIDTH - back, :]
    tail_ref[...] = u[L - SUBLANES:L, :]
    yield
    xc = _silu(acc)
    yield
    xs = xc[:, :SSD_WIDTH]
    xsb = xs.astype(BF16)
    bmat = xc[:, SSD_WIDTH:SSD_WIDTH + SSD_GROUPS * SSD_STATE]
    cmat = xc[:, SSD_WIDTH + SSD_GROUPS * SSD_STATE:]

    lane = lax.broadcasted_iota(jnp.int32, (L, LANES), 1)
    a_col = -jnp.exp(p["col_alog"][...]) * _softplus(dtf + p["col_bias"][...])
    vcol = jnp.where(lane < SSD_HEADS, a_col, 0.0)
    acol2 = _dot(p["tri3"][...], jnp.concatenate(_split3(vcol), axis=0)) * LOG2E

    row = lax.broadcasted_iota(jnp.int32, (2 * SSD_HEADS, L), 0)
    dt_t = _softplus(dtft + p["row_bias"][...])
    a_t = jnp.where(row < SSD_HEADS, -jnp.exp(p["row_alog"][...]) * dt_t, 0.0)
    acum2_t = _dot(jnp.concatenate(_split3(a_t), axis=1), p["trit3"][...]) * LOG2E
    last2_t = acum2_t[:, L - 1:L]
    w_t = jnp.exp2(last2_t - acum2_t) * dt_t
    arow_t = acum2_t - jnp.log2(dt_t)
    yield

    ri = lax.broadcasted_iota(jnp.int32, (L, L), 0)
    ci = lax.broadcasted_iota(jnp.int32, (L, L), 1)
    causal = ri >= ci

    cb, b_t, c_f = [], [], []
    for g in range(SSD_GROUPS):
        bg = bmat[:, g * SSD_STATE:(g + 1) * SSD_STATE]
        cg = cmat[:, g * SSD_STATE:(g + 1) * SSD_STATE]
        cb.append(_dot_nt(cg.astype(BF16), bg.astype(BF16)))
        b_t.append(bg.T)
        c_f.append(cg)

    g_mats, ce_mats, btw_mats, cdecay = [], [], [], []
    for h in range(SSD_HEADS):
        g = h // (SSD_HEADS // SSD_GROUPS)
        acol = jnp.broadcast_to(acol2[:, h:h + 1], (L, L))
        g_mats.append((cb[g] * jnp.where(causal, jnp.exp2(acol - arow_t[h:h + 1, :]), 0.0)).astype(BF16))
        ce_mats.append((c_f[g] * jnp.exp2(acol)).astype(BF16))
        btw_mats.append((b_t[g] * w_t[h:h + 1, :]).astype(BF16))
        cdecay.append(jnp.exp2(jnp.broadcast_to(last2_t[h:h + 1, :], (L, LANES))))
        yield

    lo_half = lane < SSD_HEAD_DIM
    st_mask = jnp.concatenate([lo_half, jnp.logical_not(lo_half)], axis=0)
    ys = []
    for i in range(N_PAIRS):
        h0, h1 = 2 * i, 2 * i + 1
        xp = xsb[:, i * LANES:(i + 1) * LANES]
        zero = jnp.zeros_like(xp)
        st = st_ref[i]
        rhs = jnp.concatenate([jnp.where(lo_half, xp, zero), jnp.where(lo_half, zero, xp),
                               st.astype(BF16)], axis=0)
        lhs = jnp.concatenate([g_mats[h0], g_mats[h1], ce_mats[h0], ce_mats[h1]], axis=1)
        ys.append(_dot(lhs, rhs))
        snew = _dot(jnp.concatenate([btw_mats[h0], btw_mats[h1]], axis=0), xp)
        cd = jnp.concatenate([cdecay[h0], cdecay[h1]], axis=0)
        st_ref[i] = jnp.where(st_mask, st * cd + snew, 0.0)
        yield

    y = jnp.concatenate(ys, axis=1) + p["d_skip"][...] * xs
    y = y * _silu(z.astype(F32))
    yield
    gw = SSD_WIDTH // SSD_GROUPS
    normed = []
    for g in range(SSD_GROUPS):
        yg = y[:, g * gw:(g + 1) * gw]
        normed.append(yg * lax.rsqrt(jnp.mean(yg * yg, axis=-1, keepdims=True) + EPS))
    y_out((jnp.concatenate(normed, axis=1) * p["norm_w"][...]).astype(BF16))


def _decay_cols_kernel(dtf_ref, cbias_ref, tri3_ref, pq_ref, pk_ref, oq_ref, ok_ref, cq_ref, ck_ref):
    L = CHUNK
    lane = lax.broadcasted_iota(jnp.int32, (L, LANES), 1)
    in_f = jnp.logical_and(lane >= SSD_HEADS, lane < 2 * SSD_HEADS)

    def body(c, carry):
        rows = pl.ds(pl.multiple_of(c * L, L), L)
        raw = dtf_ref[rows, :] + cbias_ref[...]
        logf = jnp.where(in_f, raw - _softplus(raw), 0.0)
        c_run = _dot(tri3_ref[...], jnp.concatenate(_split3(logf), axis=0)) + carry
        c3 = jnp.concatenate(_split3(c_run * LOG2E), axis=1)
        cq_ref[rows, :] = (_dot(c3, pq_ref[...]) + oq_ref[...]).astype(BF16)
        ck_ref[rows, :] = (_dot(c3, pk_ref[...]) + ok_ref[...]).astype(BF16)
        return c_run[L - 1:L, :]

    lax.fori_loop(0, dtf_ref.shape[0] // L, body, jnp.zeros((1, LANES), F32))


def _decay_cols(dtf, consts, bsz, seq):
    m = dtf.shape[0]
    blk = lambda: pl.BlockSpec((seq, LANES), lambda b: (b, 0))
    return pl.pallas_call(
        _decay_cols_kernel,
        grid=(bsz,),
        in_specs=[blk()] + [_const_spec(a.shape) for a in consts],
        out_specs=(blk(), blk()),
        out_shape=(jax.ShapeDtypeStruct((m, LANES), BF16), jax.ShapeDtypeStruct((m, LANES), BF16)),
        compiler_params=pltpu.CompilerParams(dimension_semantics=("arbitrary",),
                                             vmem_limit_bytes=VMEM_LIMIT_BYTES),
        name="decay_cols",
    )(dtf, *consts)


def _attn_kernel(q_ref, k_ref, vt_ref, cq_ref, ck_ref, o_ref, qx_ref, m_ref, acc_ref):
    t = ATT_TILE
    n_tiles = q_ref.shape[1] // t
    pair = pl.program_id(1)
    cc = ATT_QUERY_CHUNK
    d = ATT_HEAD_DIM

    lane = lax.broadcasted_iota(jnp.int32, (t, LANES), 1)
    lo_half = lane < d
    for qi in range(n_tiles):
        qs = q_ref[0, qi * t:(qi + 1) * t, :]
        cqv = cq_ref[0, qi * t:(qi + 1) * t, :]
        zero = jnp.zeros_like(qs)
        for hh in range(HEADS_PER_LANE_TILE):
            head = pair * HEADS_PER_LANE_TILE + hh
            qm = jnp.where(lo_half, qs, zero) if hh == 0 else jnp.where(lo_half, zero, qs)
            cm = jnp.where((lane // BIAS_LANES_PER_HEAD) == head, cqv, zero)
            qx_ref[qi, hh * t:(hh + 1) * t, :] = jnp.concatenate([qm, cm], axis=1)

    def ones_rows(nk):
        return jnp.where(lax.broadcasted_iota(jnp.int32, (ATT_DENOM_ROWS, nk), 0) == 0, 1.0, 0.0).astype(BF16)

    def scores(qi, j):
        kx = jnp.concatenate([k_ref[0, j * t:(j + 1) * t, :], ck_ref[0, j * t:(j + 1) * t, :]], axis=1)
        return _dot_nt(kx, qx_ref[qi])

    def softmax_pv(s_all, qi, j):
        first = j == qi
        for c in range(HEADS_PER_LANE_TILE * t // cc):
            cols = slice(c * cc, (c + 1) * cc)
            hh = (c * cc) // t
            off = (c * cc) % t
            nk = off + cc if first else t
            vh = jnp.concatenate([vt_ref[hh * d:(hh + 1) * d, j * t:j * t + nk], ones_rows(nk)], axis=0)
            s = s_all[:nk, cols]
            if first:
                key_i = lax.broadcasted_iota(jnp.int32, (nk, cc), 0)
                qry_i = lax.broadcasted_iota(jnp.int32, (nk, cc), 1)
                s = jnp.where(key_i <= qry_i + off, s, -jnp.inf)
                m_new = jnp.max(s, axis=0, keepdims=True)
                p = jnp.exp2(s - m_new)
                acc_ref[qi, :, cols] = _dot(vh, p.astype(BF16))
            else:
                m_old = m_ref[qi, 0:1, cols]
                m_new = jnp.maximum(m_old, jnp.max(s, axis=0, keepdims=True))
                alpha = jnp.exp2(m_old - m_new)
                p = jnp.exp2(s - m_new)
                acc_ref[qi, :, cols] = alpha * acc_ref[qi, :, cols] + _dot(vh, p.astype(BF16))
            m_ref[qi, 0:1, cols] = m_new

    for qi in range(n_tiles):
        softmax_pv(scores(qi, qi), qi, qi)
    for qi in range(n_tiles):
        for j in range(qi):
            softmax_pv(scores(qi, j), qi, j)
        o_t = jnp.concatenate([acc_ref[qi, 0:d, hh * t:(hh + 1) * t] / acc_ref[qi, d:d + 1, hh * t:(hh + 1) * t]
                               for hh in range(HEADS_PER_LANE_TILE)], axis=0)
        o_ref[0, qi * t:(qi + 1) * t, :] = o_t.T.astype(BF16)


def _attention(q3, k3, vt, cq3, ck3):
    bsz, seq, _ = q3.shape
    t = ATT_TILE
    n_tiles = seq // t
    cols = HEADS_PER_LANE_TILE * t
    seq_block = lambda lane_block: pl.BlockSpec((1, seq, LANES), lane_block)
    return pl.pallas_call(
        _attn_kernel,
        grid=(bsz, N_PAIRS),
        in_specs=[seq_block(lambda b, p: (b, 0, p)),
                  seq_block(lambda b, p: (b, 0, p)),
                  pl.BlockSpec((LANES, seq), lambda b, p: (p, b)),
                  seq_block(lambda b, p: (b, 0, 0)),
                  seq_block(lambda b, p: (b, 0, 0))],
        out_specs=seq_block(lambda b, p: (b, 0, p)),
        out_shape=jax.ShapeDtypeStruct((bsz, seq, ATT_WIDTH), BF16),
        scratch_shapes=[pltpu.VMEM((n_tiles, cols, 2 * LANES), BF16),
                        pltpu.VMEM((n_tiles, SUBLANES, cols), F32),
                        pltpu.VMEM((n_tiles, ATT_HEAD_DIM + ATT_DENOM_ROWS, cols), F32)],
        compiler_params=pltpu.CompilerParams(dimension_semantics=("arbitrary",) * 2,
                                             vmem_limit_bytes=VMEM_LIMIT_BYTES),
        name="fox_attn",
    )(q3, k3, vt, cq3, ck3)


_SSD_PARAMS = ("conv_w", "conv_b", "col_bias", "col_alog", "row_bias", "row_alog", "d_skip", "norm_w",
               "tri3", "trit3")


def _ssd_mlp_kernel(*refs, tiles_per_batch):
    xbc_ref, z_ref, dtf_ref, dtft_ref, x_ref, ya_ref = refs[:6]
    n_ssd = len(_SSD_PARAMS)
    p = dict(zip(_SSD_PARAMS, refs[6:6 + n_ssd]))
    wos_ref, woa_ref, nmw_ref, wup_ref, wdn_ref, nfw_ref = refs[6 + n_ssd:12 + n_ssd]
    o_ref, tail_ref, st_ref, ys_ref, ys_next_ref = refs[12 + n_ssd:]
    i = pl.program_id(0)

    @pl.when(i % tiles_per_batch == 0)
    def _():
        tail_ref[...] = jnp.zeros_like(tail_ref)
        st_ref[...] = jnp.zeros_like(st_ref)

    @pl.when(i == 0)
    def _():
        ys_ref[...] = jnp.zeros_like(ys_ref)

    nb = MLP_COL_BLOCK

    def mlp_pieces():
        ys, ya = ys_ref[...], ya_ref[...]
        h1_blocks = []
        for j in range(D_MODEL // nb):
            cols = slice(j * nb, (j + 1) * nb)
            h1_blocks.append(x_ref[:, cols] + _dot(ys, wos_ref[:, cols]) + _dot(ya, woa_ref[:, cols]))
            yield
        h1 = jnp.concatenate(h1_blocks, axis=1)
        ms = jnp.mean(h1 * h1, axis=-1, keepdims=True)
        hb = ((h1 * lax.rsqrt(ms + EPS)) * nmw_ref[...]).astype(BF16)
        o_ref[...] = h1
        yield
        for c in range(D_FF // D_MODEL):
            u_blocks = []
            for j in range(D_MODEL // nb):
                cols = slice(c * D_MODEL + j * nb, c * D_MODEL + (j + 1) * nb)
                u = _dot(hb, wup_ref[:, cols])
                u_blocks.append(jnp.square(jnp.maximum(u, 0.0)).astype(BF16))
                yield
            u_all = jnp.concatenate(u_blocks, axis=1)
            for j in range(D_MODEL // nb):
                cols = slice(j * nb, (j + 1) * nb)
                o_ref[:, cols] += _dot(u_all, wdn_ref[c * D_MODEL:(c + 1) * D_MODEL, cols])
                yield
        acc = o_ref[...]
        ms2 = jnp.mean(acc * acc, axis=-1, keepdims=True)
        o_ref[...] = (acc * lax.rsqrt(ms2 + EPS)) * nfw_ref[...]

    def ssd_pieces():
        for c in range(ROW_TILE // CHUNK):
            rows = slice(c * CHUNK, (c + 1) * CHUNK)

            def store(y, rows=rows):
                ys_next_ref[rows, :] = y

            yield from _ssd_chunk(xbc_ref[rows, :], z_ref[rows, :], dtf_ref[rows, :], dtft_ref[:, rows],
                                  p, tail_ref, st_ref, store)

    _trace_round_robin(mlp_pieces(), ssd_pieces(), ratio=SSD_PIECES_PER_MLP_PIECE)
    ys_ref[...] = ys_next_ref[...]


def _trace_round_robin(a, b, ratio):
    done_a = done_b = False
    while not (done_a and done_b):
        if not done_a:
            done_a = next(a, _DONE) is _DONE
        for _ in range(ratio):
            if not done_b:
                done_b = next(b, _DONE) is _DONE


_DONE = object()


def _ssd_mlp(xbc, z, dtf, dtft, x2, ya, ssd_consts, mlp_consts, seq):
    m = x2.shape[0]
    tm = ROW_TILE
    n_tiles = m // tm
    cur = lambda i: jnp.minimum(i, n_tiles - 1)
    prv = lambda i: jnp.maximum(i - 1, 0)
    consts = tuple(ssd_consts) + tuple(mlp_consts)
    return pl.pallas_call(
        functools.partial(_ssd_mlp_kernel, tiles_per_batch=seq // tm),
        grid=(n_tiles + 1,),
        in_specs=[pl.BlockSpec((tm, CONV_CH), lambda i: (cur(i), 0)),
                  pl.BlockSpec((tm, SSD_WIDTH), lambda i: (cur(i), 0)),
                  pl.BlockSpec((tm, LANES), lambda i: (cur(i), 0)),
                  pl.BlockSpec((2 * SSD_HEADS, tm), lambda i: (0, cur(i))),
                  pl.BlockSpec((tm, D_MODEL), lambda i: (prv(i), 0)),
                  pl.BlockSpec((tm, ATT_WIDTH), lambda i: (prv(i), 0))]
                 + [_const_spec(a.shape) for a in consts],
        out_specs=pl.BlockSpec((tm, D_MODEL), lambda i: (prv(i), 0)),
        out_shape=jax.ShapeDtypeStruct((m, D_MODEL), F32),
        scratch_shapes=[pltpu.VMEM((SUBLANES, CONV_CH), F32),
                        pltpu.VMEM((N_PAIRS, 2 * CHUNK, LANES), F32),
                        pltpu.VMEM((tm, SSD_WIDTH), BF16),
                        pltpu.VMEM((tm, SSD_WIDTH), BF16)],
        compiler_params=pltpu.CompilerParams(dimension_semantics=("arbitrary",),
                                             vmem_limit_bytes=VMEM_LIMIT_BYTES),
        name="ssd_mlp",
    )(xbc, z, dtf, dtft, x2, ya, *consts)


def _ssd_constants():
    t = np.arange(CHUNK)
    tri = (t[None, :] <= t[:, None]).astype(np.float32)
    tri3 = np.concatenate([tri, tri, tri], axis=1)
    trit3 = np.concatenate([tri.T, tri.T, tri.T], axis=0)
    pq = np.zeros((3 * LANES, LANES), np.float32)
    pk = np.zeros((3 * LANES, LANES), np.float32)
    oq = np.zeros((1, LANES), np.float32)
    ok = np.zeros((1, LANES), np.float32)
    for h in range(ATT_HEADS):
        base = h * BIAS_LANES_PER_HEAD
        for j in range(3):
            src = j * LANES + SSD_HEADS + h
            pq[src, base + j] = 1.0
            pk[src, base + 3 + j] = -1.0
            ok[0, base + j] = 1.0
            oq[0, base + 3 + j] = 1.0
    return (jnp.asarray(tri3, BF16), jnp.asarray(trit3, BF16), jnp.asarray(pq, BF16), jnp.asarray(pk, BF16),
            jnp.asarray(oq), jnp.asarray(ok))


def kernel(x, norm_mix_w, w_in, conv_w, conv_b, dt_bias, a_log, d_skip, ssd_norm_w, f_bias, w_out,
           norm_mlp_w, w_up, w_down, norm_final_w):
    bsz, seq, _ = x.shape
    m = bsz * seq
    assert norm_mix_w.shape[0] == 1, "single layer"
    assert seq % ROW_TILE == 0 and seq % ATT_TILE == 0 and ROW_TILE % CHUNK == 0
    x2 = x.reshape(m, D_MODEL)

    w = w_in[0]
    o_z, o_xbc = SSD_WIDTH, SSD_WIDTH + CONV_CH
    o_dt = o_xbc + SSD_HEADS
    o_q, o_k, o_v = o_dt + ATT_WIDTH, o_dt + 2 * ATT_WIDTH, o_dt + 3 * ATT_WIDTH
    wz = w[:, :o_z].astype(BF16)
    wxbc = w[:, o_z:o_xbc].astype(BF16)
    wq = w[:, o_dt:o_q].astype(BF16)
    wk = w[:, o_q:o_k].astype(BF16)
    wvt = w[:, o_k:o_v].T.astype(BF16)
    w_dtf = jnp.concatenate([w[:, o_xbc:o_dt], w[:, o_v:]], axis=1)
    wdtf = jnp.pad(w_dtf, ((0, 0), (0, LANES - 2 * SSD_HEADS))).astype(BF16)
    wdtft = w_dtf.T.astype(BF16)

    z, xbc, q, k, vt, dtf, dtft = _in_proj(x2, norm_mix_w[0][None, :], wz, wxbc, wq, wk, wvt, wdtf, wdtft)

    pad_lanes = lambda a: jnp.pad(a, (0, LANES - a.shape[0]))[None, :]
    col_bias = pad_lanes(jnp.concatenate([dt_bias[0], f_bias[0]]))
    col_alog = pad_lanes(a_log[0])
    row_bias = jnp.broadcast_to(jnp.concatenate([dt_bias[0], f_bias[0]])[:, None], (2 * SSD_HEADS, CHUNK))
    row_alog = jnp.broadcast_to(jnp.pad(a_log[0], (0, SSD_HEADS))[:, None], (2 * SSD_HEADS, CHUNK))
    dskip = jnp.repeat(d_skip[0], SSD_HEAD_DIM)[None, :]
    tri3, trit3, pq, pk, oq, ok = _ssd_constants()
    cq, ck = _decay_cols(dtf, (col_bias, tri3, pq, pk, oq, ok), bsz, seq)

    to3 = lambda a: a.reshape(bsz, seq, a.shape[-1])
    y_att = _attention(to3(q), to3(k), vt, to3(cq), to3(ck)).reshape(m, ATT_WIDTH)

    wo = w_out[0].astype(BF16)
    ssd_consts = (conv_w[0], conv_b[0][None, :], col_bias, col_alog, row_bias, row_alog, dskip,
                  ssd_norm_w[0][None, :], tri3, trit3)
    mlp_consts = (wo[:SSD_WIDTH], wo[SSD_WIDTH:], norm_mlp_w[0][None, :], w_up[0].astype(BF16),
                  w_down[0].astype(BF16), norm_final_w[None, :])
    out = _ssd_mlp(xbc, z, dtf, dtft, x2, y_att, ssd_consts, mlp_consts, seq)
    return out.reshape(bsz, seq, D_MODEL)
```

```python
import functools

import numpy as np
import jax
import jax.numpy as jnp
from jax import lax
from jax.experimental import pallas as pl
from jax.experimental.pallas import tpu as pltpu

D_MODEL = 1024
SSD_HEADS = 16
SSD_HEAD_DIM = 64
SSD_WIDTH = SSD_HEADS * SSD_HEAD_DIM
SSD_GROUPS = 2
SSD_STATE = 128
CONV_WIDTH = 4
CHUNK = 128
CONV_CH = SSD_WIDTH + 2 * SSD_GROUPS * SSD_STATE
ATT_HEADS = 16
ATT_HEAD_DIM = 64
ATT_WIDTH = ATT_HEADS * ATT_HEAD_DIM
D_FF = 4 * D_MODEL
EPS = 1e-5

LANES = 128
SUBLANES = 8
HEADS_PER_LANE_TILE = LANES // SSD_HEAD_DIM
N_PAIRS = SSD_HEADS // HEADS_PER_LANE_TILE
BIAS_LANES_PER_HEAD = LANES // ATT_HEADS
VMEM_LIMIT_BYTES = 56 * 1024 * 1024

ROW_TILE = 512
MLP_COL_BLOCK = 256
SSD_PIECES_PER_MLP_PIECE = 3
ATT_TILE = 1024
ATT_QUERY_CHUNK = 256
ATT_DENOM_ROWS = 16
LOG2E = 1.4426950408889634
QK_SCALE = LOG2E * ATT_HEAD_DIM ** -0.5

F32 = jnp.float32
BF16 = jnp.bfloat16

_NT = (((1,), (1,)), ((), ()))


def _dot(a, b):
    return jnp.dot(a, b, preferred_element_type=F32)


def _dot_nt(a, b):
    return lax.dot_general(a, b, _NT, preferred_element_type=F32)


def _softplus(x):
    return jnp.maximum(x, 0.0) + jnp.log1p(jnp.exp(-jnp.abs(x)))


def _silu(x):
    return x / (1.0 + jnp.exp(-x))


def _split3(v):
    hi = v.astype(BF16)
    r1 = v - hi.astype(F32)
    mid = r1.astype(BF16)
    lo = (r1 - mid.astype(F32)).astype(BF16)
    return hi, mid, lo


def _const_spec(shape):
    zeros = (0,) * len(shape)
    return pl.BlockSpec(shape, lambda *_: zeros, pipeline_mode=pl.Buffered(1))


def _inproj_kernel(x_ref, nw_ref, wz_ref, wxbc_ref, wq_ref, wk_ref, wvt_ref, wdtf_ref, wdtft_ref,
                   z_ref, xbc_ref, q_ref, k_ref, vt_ref, dtf_ref, dtft_ref):
    x = x_ref[...]
    ms = jnp.mean(x * x, axis=-1, keepdims=True)
    hb = ((x * lax.rsqrt(ms + EPS)) * nw_ref[...]).astype(BF16)
    z_ref[...] = _dot(hb, wz_ref[...]).astype(BF16)
    xbc_ref[...] = _dot(hb, wxbc_ref[...]).astype(BF16)
    q_ref[...] = (_dot(hb, wq_ref[...]) * QK_SCALE).astype(BF16)
    k_ref[...] = _dot(hb, wk_ref[...]).astype(BF16)
    vt_ref[...] = _dot_nt(wvt_ref[...], hb).astype(BF16)
    dtf_ref[...] = _dot(hb, wdtf_ref[...])
    dtft_ref[...] = _dot_nt(wdtft_ref[...], hb)


def _in_proj(x2, norm_w, wz, wxbc, wq, wk, wvt, wdtf, wdtft):
    m = x2.shape[0]
    tm = ROW_TILE
    row = lambda width: pl.BlockSpec((tm, width), lambda i: (i, 0))
    out_shapes = (
        jax.ShapeDtypeStruct((m, SSD_WIDTH), BF16),
        jax.ShapeDtypeStruct((m, CONV_CH), BF16),
        jax.ShapeDtypeStruct((m, ATT_WIDTH), BF16),
        jax.ShapeDtypeStruct((m, ATT_WIDTH), BF16),
        jax.ShapeDtypeStruct((ATT_WIDTH, m), BF16),
        jax.ShapeDtypeStruct((m, LANES), F32),
        jax.ShapeDtypeStruct((2 * SSD_HEADS, m), F32),
    )
    return pl.pallas_call(
        _inproj_kernel,
        grid=(m // tm,),
        in_specs=[row(D_MODEL), _const_spec(norm_w.shape), _const_spec(wz.shape), _const_spec(wxbc.shape),
                  _const_spec(wq.shape), _const_spec(wk.shape), _const_spec(wvt.shape),
                  _const_spec(wdtf.shape), _const_spec(wdtft.shape)],
        out_specs=(row(SSD_WIDTH), row(CONV_CH), row(ATT_WIDTH), row(ATT_WIDTH),
                   pl.BlockSpec((ATT_WIDTH, tm), lambda i: (0, i)), row(LANES),
                   pl.BlockSpec((2 * SSD_HEADS, tm), lambda i: (0, i))),
        out_shape=out_shapes,
        compiler_params=pltpu.CompilerParams(dimension_semantics=("arbitrary",),
                                             vmem_limit_bytes=VMEM_LIMIT_BYTES),
        name="in_proj",
    )(x2, norm_w, wz, wxbc, wq, wk, wvt, wdtf, wdtft)


def _ssd_chunk(xbc, z, dtf, dtft, p, tail_ref, st_ref, y_out):
    L = CHUNK

    u = xbc.astype(F32)
    ext = jnp.concatenate([tail_ref[...], u], axis=0)
    w = p["conv_w"][...]
    acc = p["conv_b"][...] + u * w[CONV_WIDTH - 1:CONV_WIDTH, :]
    for back in range(1, CONV_WIDTH):
        shifted = pltpu.roll(ext, back, axis=0)[SUBLANES:SUBLANES + L, :]
        acc = acc + shifted * w[CONV_WIDTH - 1 - back:CONV_WIDTH - back, :]
    tail_ref[...] = u[L - SUBLANES:L, :]
    yield
    xc = _silu(acc)
    yield
    xs = xc[:, :SSD_WIDTH]
    xsb = xs.astype(BF16)
    bmat = xc[:, SSD_WIDTH:SSD_WIDTH + SSD_GROUPS * SSD_STATE]
    cmat = xc[:, SSD_WIDTH + SSD_GROUPS * SSD_STATE:]

    lane = lax.broadcasted_iota(jnp.int32, (L, LANES), 1)
    a_col = -jnp.exp(p["col_alog"][...]) * _softplus(dtf + p["col_bias"][...])
    vcol = jnp.where(lane < SSD_HEADS, a_col, 0.0)
    acol2 = _dot(p["tri3"][...], jnp.concatenate(_split3(vcol), axis=0)) * LOG2E

    row = lax.broadcasted_iota(jnp.int32, (2 * SSD_HEADS, L), 0)
    dt_t = _softplus(dtft + p["row_bias"][...])
    a_t = jnp.where(row < SSD_HEADS, -jnp.exp(p["row_alog"][...]) * dt_t, 0.0)
    acum2_t = _dot(jnp.concatenate(_split3(a_t), axis=1), p["trit3"][...]) * LOG2E
    last2_t = acum2_t[:, L - 1:L]
    w_t = jnp.exp2(last2_t - acum2_t) * dt_t
    arow_t = acum2_t - jnp.log2(dt_t)
    yield

    ri = lax.broadcasted_iota(jnp.int32, (L, L), 0)
    ci = lax.broadcasted_iota(jnp.int32, (L, L), 1)
    causal = ri >= ci

    cb, b_t, c_f = [], [], []
    for g in range(SSD_GROUPS):
        bg = bmat[:, g * SSD_STATE:(g + 1) * SSD_STATE]
        cg = cmat[:, g * SSD_STATE:(g + 1) * SSD_STATE]
        cb.append(_dot_nt(cg.astype(BF16), bg.astype(BF16)))
        b_t.append(bg.T)
        c_f.append(cg)

    g_mats, ce_mats, btw_mats, cdecay = [], [], [], []
    for h in range(SSD_HEADS):
        g = h // (SSD_HEADS // SSD_GROUPS)
        acol = jnp.broadcast_to(acol2[:, h:h + 1], (L, L))
        g_mats.append((cb[g] * jnp.where(causal, jnp.exp2(acol - arow_t[h:h + 1, :]), 0.0)).astype(BF16))
        ce_mats.append((c_f[g] * jnp.exp2(acol)).astype(BF16))
        btw_mats.append((b_t[g] * w_t[h:h + 1, :]).astype(BF16))
        cdecay.append(jnp.exp2(jnp.broadcast_to(last2_t[h:h + 1, :], (L, LANES))))
        yield

    lo_half = lane < SSD_HEAD_DIM
    st_mask = jnp.concatenate([lo_half, jnp.logical_not(lo_half)], axis=0)
    ys = []
    for i in range(N_PAIRS):
        h0, h1 = 2 * i, 2 * i + 1
        xp = xsb[:, i * LANES:(i + 1) * LANES]
        zero = jnp.zeros_like(xp)
        st = st_ref[i]
        rhs = jnp.concatenate([jnp.where(lo_half, xp, zero), jnp.where(lo_half, zero, xp),
                               st.astype(BF16)], axis=0)
        lhs = jnp.concatenate([g_mats[h0], g_mats[h1], ce_mats[h0], ce_mats[h1]], axis=1)
        ys.append(_dot(lhs, rhs))
        snew = _dot(jnp.concatenate([btw_mats[h0], btw_mats[h1]], axis=0), xp)
        cd = jnp.concatenate([cdecay[h0], cdecay[h1]], axis=0)
        st_ref[i] = jnp.where(st_mask, st * cd + snew, 0.0)
        yield

    y = jnp.concatenate(ys, axis=1) + p["d_skip"][...] * xs
    y = y * _silu(z.astype(F32))
    yield
    gw = SSD_WIDTH // SSD_GROUPS
    normed = []
    for g in range(SSD_GROUPS):
        yg = y[:, g * gw:(g + 1) * gw]
        normed.append(yg * lax.rsqrt(jnp.mean(yg * yg, axis=-1, keepdims=True) + EPS))
    y_out((jnp.concatenate(normed, axis=1) * p["norm_w"][...]).astype(BF16))


def _decay_cols_kernel(dtf_ref, cbias_ref, tri3_ref, pq_ref, pk_ref, oq_ref, ok_ref, cq_ref, ck_ref):
    L = CHUNK
    lane = lax.broadcasted_iota(jnp.int32, (L, LANES), 1)
    in_f = jnp.logical_and(lane >= SSD_HEADS, lane < 2 * SSD_HEADS)

    carry = jnp.zeros((1, LANES), F32)
    for c in range(dtf_ref.shape[0] // L):
        rows = slice(c * L, (c + 1) * L)
        raw = dtf_ref[rows, :] + cbias_ref[...]
        logf = jnp.where(in_f, raw - _softplus(raw), 0.0)
        c_run = _dot(tri3_ref[...], jnp.concatenate(_split3(logf), axis=0)) + carry
        c3 = jnp.concatenate(_split3(c_run * LOG2E), axis=1)
        cq_ref[rows, :] = (_dot(c3, pq_ref[...]) + oq_ref[...]).astype(BF16)
        ck_ref[rows, :] = (_dot(c3, pk_ref[...]) + ok_ref[...]).astype(BF16)
        carry = c_run[L - 1:L, :]


def _decay_cols(dtf, consts, bsz, seq):
    m = dtf.shape[0]
    blk = lambda: pl.BlockSpec((seq, LANES), lambda b: (b, 0))
    return pl.pallas_call(
        _decay_cols_kernel,
        grid=(bsz,),
        in_specs=[blk()] + [_const_spec(a.shape) for a in consts],
        out_specs=(blk(), blk()),
        out_shape=(jax.ShapeDtypeStruct((m, LANES), BF16), jax.ShapeDtypeStruct((m, LANES), BF16)),
        compiler_params=pltpu.CompilerParams(dimension_semantics=("arbitrary",),
                                             vmem_limit_bytes=VMEM_LIMIT_BYTES),
        name="decay_cols",
    )(dtf, *consts)


def _attn_kernel(q_ref, k_ref, vt_ref, cq_ref, ck_ref, o_ref, qx_ref, m_ref, acc_ref):
    t = ATT_TILE
    n_tiles = q_ref.shape[1] // t
    pair = pl.program_id(1)
    cc = ATT_QUERY_CHUNK
    d = ATT_HEAD_DIM

    lane = lax.broadcasted_iota(jnp.int32, (t, LANES), 1)
    lo_half = lane < d
    for qi in range(n_tiles):
        qs = q_ref[0, qi * t:(qi + 1) * t, :]
        cqv = cq_ref[0, qi * t:(qi + 1) * t, :]
        zero = jnp.zeros_like(qs)
        for hh in range(HEADS_PER_LANE_TILE):
            head = pair * HEADS_PER_LANE_TILE + hh
            qm = jnp.where(lo_half, qs, zero) if hh == 0 else jnp.where(lo_half, zero, qs)
            cm = jnp.where((lane // BIAS_LANES_PER_HEAD) == head, cqv, zero)
            qx_ref[qi, hh * t:(hh + 1) * t, :] = jnp.concatenate([qm, cm], axis=1)

    def ones_rows(nk):
        return jnp.where(lax.broadcasted_iota(jnp.int32, (ATT_DENOM_ROWS, nk), 0) == 0, 1.0, 0.0).astype(BF16)

    n_chunks = HEADS_PER_LANE_TILE * t // cc

    def keys_needed(qi, j, c):
        return (c * cc) % t + cc if j == qi else t

    def logits(qi, j, out):
        kx = jnp.concatenate([k_ref[0, j * t:(j + 1) * t, :], ck_ref[0, j * t:(j + 1) * t, :]], axis=1)
        for c in range(n_chunks):
            out[c] = _dot_nt(kx[:keys_needed(qi, j, c)], qx_ref[qi, c * cc:(c + 1) * cc, :])
            yield

    def softmax_pv(qi, j, s_chunks):
        first = j == qi
        for c in range(n_chunks):
            cols = slice(c * cc, (c + 1) * cc)
            hh = (c * cc) // t
            off = (c * cc) % t
            nk = keys_needed(qi, j, c)
            vh = jnp.concatenate([vt_ref[hh * d:(hh + 1) * d, j * t:j * t + nk], ones_rows(nk)], axis=0)
            s = s_chunks[c]
            if first:
                key_i = lax.broadcasted_iota(jnp.int32, (nk, cc), 0)
                qry_i = lax.broadcasted_iota(jnp.int32, (nk, cc), 1)
                s = jnp.where(key_i <= qry_i + off, s, -jnp.inf)
                m_new = jnp.max(s, axis=0, keepdims=True)
                p = jnp.exp2(s - m_new)
                acc_ref[qi, :, cols] = _dot(vh, p.astype(BF16))
            else:
                m_old = m_ref[qi, 0:1, cols]
                m_new = jnp.maximum(m_old, jnp.max(s, axis=0, keepdims=True))
                alpha = jnp.exp2(m_old - m_new)
                p = jnp.exp2(s - m_new)
                acc_ref[qi, :, cols] = alpha * acc_ref[qi, :, cols] + _dot(vh, p.astype(BF16))
            m_ref[qi, 0:1, cols] = m_new
            yield
        if j == (qi - 1 if qi else 0):
            o_t = jnp.concatenate([acc_ref[qi, 0:d, hh * t:(hh + 1) * t] / acc_ref[qi, d:d + 1, hh * t:(hh + 1) * t]
                                   for hh in range(HEADS_PER_LANE_TILE)], axis=0)
            o_ref[0, qi * t:(qi + 1) * t, :] = o_t.T.astype(BF16)

    products = [(qi, qi) for qi in range(n_tiles)] + [(qi, j) for qi in range(n_tiles) for j in range(qi)]
    s_chunks = [dict() for _ in products]
    _trace_round_robin(logits(*products[0], s_chunks[0]), iter(()), ratio=1)
    for n, (qi, j) in enumerate(products):
        nxt = logits(*products[n + 1], s_chunks[n + 1]) if n + 1 < len(products) else iter(())
        _trace_round_robin(nxt, softmax_pv(qi, j, s_chunks[n]), ratio=1)


def _attention(q3, k3, vt, cq3, ck3):
    bsz, seq, _ = q3.shape
    t = ATT_TILE
    n_tiles = seq // t
    cols = HEADS_PER_LANE_TILE * t
    seq_block = lambda lane_block: pl.BlockSpec((1, seq, LANES), lane_block)
    return pl.pallas_call(
        _attn_kernel,
        grid=(bsz, N_PAIRS),
        in_specs=[seq_block(lambda b, p: (b, 0, p)),
                  seq_block(lambda b, p: (b, 0, p)),
                  pl.BlockSpec((LANES, seq), lambda b, p: (p, b)),
                  seq_block(lambda b, p: (b, 0, 0)),
                  seq_block(lambda b, p: (b, 0, 0))],
        out_specs=seq_block(lambda b, p: (b, 0, p)),
        out_shape=jax.ShapeDtypeStruct((bsz, seq, ATT_WIDTH), BF16),
        scratch_shapes=[pltpu.VMEM((n_tiles, cols, 2 * LANES), BF16),
                        pltpu.VMEM((n_tiles, SUBLANES, cols), F32),
                        pltpu.VMEM((n_tiles, ATT_HEAD_DIM + ATT_DENOM_ROWS, cols), F32)],
        compiler_params=pltpu.CompilerParams(dimension_semantics=("arbitrary",) * 2,
                                             vmem_limit_bytes=VMEM_LIMIT_BYTES),
        name="fox_attn",
    )(q3, k3, vt, cq3, ck3)


_SSD_PARAMS = ("conv_w", "conv_b", "col_bias", "col_alog", "row_bias", "row_alog", "d_skip", "norm_w",
               "tri3", "trit3")


def _ssd_mlp_kernel(*refs, tiles_per_batch):
    xbc_ref, z_ref, dtf_ref, dtft_ref, x_ref, ya_ref = refs[:6]
    n_ssd = len(_SSD_PARAMS)
    p = dict(zip(_SSD_PARAMS, refs[6:6 + n_ssd]))
    wos_ref, woa_ref, nmw_ref, wup_ref, wdn_ref, nfw_ref = refs[6 + n_ssd:12 + n_ssd]
    o_ref, tail_ref, st_ref, ys_ref, ys_next_ref = refs[12 + n_ssd:]
    i = pl.program_id(0)

    @pl.when(i % tiles_per_batch == 0)
    def _():
        tail_ref[...] = jnp.zeros_like(tail_ref)
        st_ref[...] = jnp.zeros_like(st_ref)

    @pl.when(i == 0)
    def _():
        ys_ref[...] = jnp.zeros_like(ys_ref)

    nb = MLP_COL_BLOCK

    def mlp_pieces():
        ys, ya = ys_ref[...], ya_ref[...]
        h1_blocks = []
        for j in range(D_MODEL // nb):
            cols = slice(j * nb, (j + 1) * nb)
            h1_blocks.append(x_ref[:, cols] + _dot(ys, wos_ref[:, cols]) + _dot(ya, woa_ref[:, cols]))
            yield
        h1 = jnp.concatenate(h1_blocks, axis=1)
        ms = jnp.mean(h1 * h1, axis=-1, keepdims=True)
        hb = ((h1 * lax.rsqrt(ms + EPS)) * nmw_ref[...]).astype(BF16)
        o_ref[...] = h1
        yield
        for c in range(D_FF // D_MODEL):
            u_blocks = []
            for j in range(D_MODEL // nb):
                cols = slice(c * D_MODEL + j * nb, c * D_MODEL + (j + 1) * nb)
                u = _dot(hb, wup_ref[:, cols])
                u_blocks.append(jnp.square(jnp.maximum(u, 0.0)).astype(BF16))
                yield
            u_all = jnp.concatenate(u_blocks, axis=1)
            for j in range(D_MODEL // nb):
                cols = slice(j * nb, (j + 1) * nb)
                o_ref[:, cols] += _dot(u_all, wdn_ref[c * D_MODEL:(c + 1) * D_MODEL, cols])
                yield
        acc = o_ref[...]
        ms2 = jnp.mean(acc * acc, axis=-1, keepdims=True)
        o_ref[...] = (acc * lax.rsqrt(ms2 + EPS)) * nfw_ref[...]

    def ssd_pieces():
        for c in range(ROW_TILE // CHUNK):
            rows = slice(c * CHUNK, (c + 1) * CHUNK)

            def store(y, rows=rows):
                ys_next_ref[rows, :] = y

            yield from _ssd_chunk(xbc_ref[rows, :], z_ref[rows, :], dtf_ref[rows, :], dtft_ref[:, rows],
                                  p, tail_ref, st_ref, store)

    _trace_round_robin(mlp_pieces(), ssd_pieces(), ratio=SSD_PIECES_PER_MLP_PIECE)
    ys_ref[...] = ys_next_ref[...]


def _trace_round_robin(a, b, ratio):
    done_a = done_b = False
    while not (done_a and done_b):
        if not done_a:
            done_a = next(a, _DONE) is _DONE
        for _ in range(ratio):
            if not done_b:
                done_b = next(b, _DONE) is _DONE


_DONE = object()


def _ssd_mlp(xbc, z, dtf, dtft, x2, ya, ssd_consts, mlp_consts, seq):
    m = x2.shape[0]
    tm = ROW_TILE
    n_tiles = m // tm
    cur = lambda i: jnp.minimum(i, n_tiles - 1)
    prv = lambda i: jnp.maximum(i - 1, 0)
    consts = tuple(ssd_consts) + tuple(mlp_consts)
    return pl.pallas_call(
        functools.partial(_ssd_mlp_kernel, tiles_per_batch=seq // tm),
        grid=(n_tiles + 1,),
        in_specs=[pl.BlockSpec((tm, CONV_CH), lambda i: (cur(i), 0)),
                  pl.BlockSpec((tm, SSD_WIDTH), lambda i: (cur(i), 0)),
                  pl.BlockSpec((tm, LANES), lambda i: (cur(i), 0)),
                  pl.BlockSpec((2 * SSD_HEADS, tm), lambda i: (0, cur(i))),
                  pl.BlockSpec((tm, D_MODEL), lambda i: (prv(i), 0)),
                  pl.BlockSpec((tm, ATT_WIDTH), lambda i: (prv(i), 0))]
                 + [_const_spec(a.shape) for a in consts],
        out_specs=pl.BlockSpec((tm, D_MODEL), lambda i: (prv(i), 0)),
        out_shape=jax.ShapeDtypeStruct((m, D_MODEL), F32),
        scratch_shapes=[pltpu.VMEM((SUBLANES, CONV_CH), F32),
                        pltpu.VMEM((N_PAIRS, 2 * CHUNK, LANES), F32),
                        pltpu.VMEM((tm, SSD_WIDTH), BF16),
                        pltpu.VMEM((tm, SSD_WIDTH), BF16)],
        compiler_params=pltpu.CompilerParams(dimension_semantics=("arbitrary",),
                                             vmem_limit_bytes=VMEM_LIMIT_BYTES),
        name="ssd_mlp",
    )(xbc, z, dtf, dtft, x2, ya, *consts)


def _ssd_constants():
    t = np.arange(CHUNK)
    tri = (t[None, :] <= t[:, None]).astype(np.float32)
    tri3 = np.concatenate([tri, tri, tri], axis=1)
    trit3 = np.concatenate([tri.T, tri.T, tri.T], axis=0)
    pq = np.zeros((3 * LANES, LANES), np.float32)
    pk = np.zeros((3 * LANES, LANES), np.float32)
    oq = np.zeros((1, LANES), np.float32)
    ok = np.zeros((1, LANES), np.float32)
    for h in range(ATT_HEADS):
        base = h * BIAS_LANES_PER_HEAD
        for j in range(3):
            src = j * LANES + SSD_HEADS + h
            pq[src, base + j] = 1.0
            pk[src, base + 3 + j] = -1.0
            ok[0, base + j] = 1.0
            oq[0, base + 3 + j] = 1.0
    return (jnp.asarray(tri3, BF16), jnp.asarray(trit3, BF16), jnp.asarray(pq, BF16), jnp.asarray(pk, BF16),
            jnp.asarray(oq), jnp.asarray(ok))


def kernel(x, norm_mix_w, w_in, conv_w, conv_b, dt_bias, a_log, d_skip, ssd_norm_w, f_bias, w_out,
           norm_mlp_w, w_up, w_down, norm_final_w):
    bsz, seq, _ = x.shape
    m = bsz * seq
    assert norm_mix_w.shape[0] == 1, "single layer"
    assert seq % ROW_TILE == 0 and seq % ATT_TILE == 0 and ROW_TILE % CHUNK == 0
    x2 = x.reshape(m, D_MODEL)

    w = w_in[0]
    o_z, o_xbc = SSD_WIDTH, SSD_WIDTH + CONV_CH
    o_dt = o_xbc + SSD_HEADS
    o_q, o_k, o_v = o_dt + ATT_WIDTH, o_dt + 2 * ATT_WIDTH, o_dt + 3 * ATT_WIDTH
    wz = w[:, :o_z].astype(BF16)
    wxbc = w[:, o_z:o_xbc].astype(BF16)
    wq = w[:, o_dt:o_q].astype(BF16)
    wk = w[:, o_q:o_k].astype(BF16)
    wvt = w[:, o_k:o_v].T.astype(BF16)
    w_dtf = jnp.concatenate([w[:, o_xbc:o_dt], w[:, o_v:]], axis=1)
    wdtf = jnp.pad(w_dtf, ((0, 0), (0, LANES - 2 * SSD_HEADS))).astype(BF16)
    wdtft = w_dtf.T.astype(BF16)

    z, xbc, q, k, vt, dtf, dtft = _in_proj(x2, norm_mix_w[0][None, :], wz, wxbc, wq, wk, wvt, wdtf, wdtft)

    pad_lanes = lambda a: jnp.pad(a, (0, LANES - a.shape[0]))[None, :]
    col_bias = pad_lanes(jnp.concatenate([dt_bias[0], f_bias[0]]))
    col_alog = pad_lanes(a_log[0])
    row_bias = jnp.broadcast_to(jnp.concatenate([dt_bias[0], f_bias[0]])[:, None], (2 * SSD_HEADS, CHUNK))
    row_alog = jnp.broadcast_to(jnp.pad(a_log[0], (0, SSD_HEADS))[:, None], (2 * SSD_HEADS, CHUNK))
    dskip = jnp.repeat(d_skip[0], SSD_HEAD_DIM)[None, :]
    tri3, trit3, pq, pk, oq, ok = _ssd_constants()
    cq, ck = _decay_cols(dtf, (col_bias, tri3, pq, pk, oq, ok), bsz, seq)

    to3 = lambda a: a.reshape(bsz, seq, a.shape[-1])
    y_att = _attention(to3(q), to3(k), vt, to3(cq), to3(ck)).reshape(m, ATT_WIDTH)

    wo = w_out[0].astype(BF16)
    ssd_consts = (conv_w[0], conv_b[0][None, :], col_bias, col_alog, row_bias, row_alog, dskip,
                  ssd_norm_w[0][None, :], tri3, trit3)
    mlp_consts = (wo[:SSD_WIDTH], wo[SSD_WIDTH:], norm_mlp_w[0][None, :], w_up[0].astype(BF16),
                  w_down[0].astype(BF16), norm_final_w[None, :])
    out = _ssd_mlp(xbc, z, dtf, dtft, x2, y_att, ssd_consts, mlp_consts, seq)
    return out.reshape(bsz, seq, D_MODEL)
```

```python
import functools

import numpy as np
import jax
import jax.numpy as jnp
from jax import lax
from jax.experimental import pallas as pl
from jax.experimental.pallas import tpu as pltpu

D_MODEL = 1024
SSD_HEADS = 16
SSD_HEAD_DIM = 64
SSD_WIDTH = SSD_HEADS * SSD_HEAD_DIM
SSD_GROUPS = 2
SSD_STATE = 128
CONV_WIDTH = 4
CHUNK = 128
CONV_CH = SSD_WIDTH + 2 * SSD_GROUPS * SSD_STATE
ATT_HEADS = 16
ATT_HEAD_DIM = 64
ATT_WIDTH = ATT_HEADS * ATT_HEAD_DIM
D_FF = 4 * D_MODEL
EPS = 1e-5

LANES = 128
SUBLANES = 8
HEADS_PER_LANE_TILE = LANES // SSD_HEAD_DIM
N_PAIRS = SSD_HEADS // HEADS_PER_LANE_TILE
BIAS_LANES_PER_HEAD = LANES // ATT_HEADS
VMEM_LIMIT_BYTES = 56 * 1024 * 1024

ROW_TILE = 512
PROJ_COL_BLOCK = 512
MLP_COL_BLOCK = 256
SSD_PIECES_PER_MLP_PIECE = 3
ATT_TILE = 1024
ATT_QUERY_CHUNK = 256
ATT_DENOM_ROWS = 16
LOG2E = 1.4426950408889634
QK_SCALE = LOG2E * ATT_HEAD_DIM ** -0.5

F32 = jnp.float32
BF16 = jnp.bfloat16

_NT = (((1,), (1,)), ((), ()))


def _dot(a, b):
    return jnp.dot(a, b, preferred_element_type=F32)


def _dot_nt(a, b):
    return lax.dot_general(a, b, _NT, preferred_element_type=F32)


def _softplus(x):
    return jnp.maximum(x, 0.0) + jnp.log1p(jnp.exp(-jnp.abs(x)))


def _silu(x):
    return x / (1.0 + jnp.exp(-x))


def _split3(v):
    hi = v.astype(BF16)
    r1 = v - hi.astype(F32)
    mid = r1.astype(BF16)
    lo = (r1 - mid.astype(F32)).astype(BF16)
    return hi, mid, lo


def _const_spec(shape):
    zeros = (0,) * len(shape)
    return pl.BlockSpec(shape, lambda *_: zeros, pipeline_mode=pl.Buffered(1))


def _inproj_kernel(x_ref, nw_ref, wz_ref, wxbc_ref, wq_ref, wk_ref, wvt_ref, wdtf_ref, wdtft_ref,
                   cbias_ref, tri3_ref, pq_ref, pk_ref, oq_ref, ok_ref,
                   z_ref, xbc_ref, q_ref, k_ref, vt_ref, dtf_ref, dtft_ref, cq_ref, ck_ref, carry_ref,
                   *, tiles_per_batch):
    L = CHUNK
    nb = PROJ_COL_BLOCK

    @pl.when(pl.program_id(0) % tiles_per_batch == 0)
    def _():
        carry_ref[...] = jnp.zeros_like(carry_ref)

    x = x_ref[...]
    ms = jnp.mean(x * x, axis=-1, keepdims=True)
    hb = ((x * lax.rsqrt(ms + EPS)) * nw_ref[...]).astype(BF16)
    dtf = _dot(hb, wdtf_ref[...])
    dtf_ref[...] = dtf

    def proj_pieces():
        for w_ref, o_ref, scale in ((wz_ref, z_ref, None), (wxbc_ref, xbc_ref, None),
                                    (wq_ref, q_ref, QK_SCALE), (wk_ref, k_ref, None)):
            for j in range(w_ref.shape[1] // nb):
                cols = slice(j * nb, (j + 1) * nb)
                y = _dot(hb, w_ref[:, cols])
                o_ref[:, cols] = (y if scale is None else y * scale).astype(BF16)
                yield
        for j in range(wvt_ref.shape[0] // nb):
            rows = slice(j * nb, (j + 1) * nb)
            vt_ref[rows, :] = _dot_nt(wvt_ref[rows, :], hb).astype(BF16)
            yield
        dtft_ref[...] = _dot_nt(wdtft_ref[...], hb)

    def decay_pieces():
        lane = lax.broadcasted_iota(jnp.int32, (L, LANES), 1)
        in_f = jnp.logical_and(lane >= SSD_HEADS, lane < 2 * SSD_HEADS)
        carry = carry_ref[0:1, :]
        for c in range(dtf.shape[0] // L):
            rows = slice(c * L, (c + 1) * L)
            raw = dtf[rows, :] + cbias_ref[...]
            logf = jnp.where(in_f, raw - _softplus(raw), 0.0)
            c_run = _dot(tri3_ref[...], jnp.concatenate(_split3(logf), axis=0)) + carry
            carry = c_run[L - 1:L, :]
            yield
            c3 = jnp.concatenate(_split3(c_run * LOG2E), axis=1)
            cq_ref[rows, :] = (_dot(c3, pq_ref[...]) + oq_ref[...]).astype(BF16)
            ck_ref[rows, :] = (_dot(c3, pk_ref[...]) + ok_ref[...]).astype(BF16)
            yield
        carry_ref[0:1, :] = carry

    _trace_round_robin(proj_pieces(), decay_pieces(), ratio=1)


def _in_proj(x2, norm_w, wz, wxbc, wq, wk, wvt, wdtf, wdtft, decay_consts, seq):
    m = x2.shape[0]
    tm = ROW_TILE
    row = lambda width: pl.BlockSpec((tm, width), lambda i: (i, 0))
    out_shapes = (
        jax.ShapeDtypeStruct((m, SSD_WIDTH), BF16),
        jax.ShapeDtypeStruct((m, CONV_CH), BF16),
        jax.ShapeDtypeStruct((m, ATT_WIDTH), BF16),
        jax.ShapeDtypeStruct((m, ATT_WIDTH), BF16),
        jax.ShapeDtypeStruct((ATT_WIDTH, m), BF16),
        jax.ShapeDtypeStruct((m, LANES), F32),
        jax.ShapeDtypeStruct((2 * SSD_HEADS, m), F32),
        jax.ShapeDtypeStruct((m, LANES), BF16),
        jax.ShapeDtypeStruct((m, LANES), BF16),
    )
    consts = (norm_w, wz, wxbc, wq, wk, wvt, wdtf, wdtft) + tuple(decay_consts)
    return pl.pallas_call(
        functools.partial(_inproj_kernel, tiles_per_batch=seq // tm),
        grid=(m // tm,),
        in_specs=[row(D_MODEL)] + [_const_spec(a.shape) for a in consts],
        out_specs=(row(SSD_WIDTH), row(CONV_CH), row(ATT_WIDTH), row(ATT_WIDTH),
                   pl.BlockSpec((ATT_WIDTH, tm), lambda i: (0, i)), row(LANES),
                   pl.BlockSpec((2 * SSD_HEADS, tm), lambda i: (0, i)), row(LANES), row(LANES)),
        out_shape=out_shapes,
        scratch_shapes=[pltpu.VMEM((SUBLANES, LANES), F32)],
        compiler_params=pltpu.CompilerParams(dimension_semantics=("arbitrary",),
                                             vmem_limit_bytes=VMEM_LIMIT_BYTES),
        name="in_proj",
    )(x2, *consts)


def _ssd_chunk(xbc, z, dtf, dtft, p, tail_ref, st_ref, y_out):
    L = CHUNK

    u = xbc.astype(F32)
    ext = jnp.concatenate([tail_ref[...], u], axis=0)
    w = p["conv_w"][...]
    acc = p["conv_b"][...] + u * w[CONV_WIDTH - 1:CONV_WIDTH, :]
    for back in range(1, CONV_WIDTH):
        shifted = pltpu.roll(ext, back, axis=0)[SUBLANES:SUBLANES + L, :]
        acc = acc + shifted * w[CONV_WIDTH - 1 - back:CONV_WIDTH - back, :]
    tail_ref[...] = u[L - SUBLANES:L, :]
    yield
    xc = _silu(acc)
    yield
    xs = xc[:, :SSD_WIDTH]
    xsb = xs.astype(BF16)
    bmat = xc[:, SSD_WIDTH:SSD_WIDTH + SSD_GROUPS * SSD_STATE]
    cmat = xc[:, SSD_WIDTH + SSD_GROUPS * SSD_STATE:]

    lane = lax.broadcasted_iota(jnp.int32, (L, LANES), 1)
    a_col = -jnp.exp(p["col_alog"][...]) * _softplus(dtf + p["col_bias"][...])
    vcol = jnp.where(lane < SSD_HEADS, a_col, 0.0)
    acol2 = _dot(p["tri3"][...], jnp.concatenate(_split3(vcol), axis=0)) * LOG2E

    row = lax.broadcasted_iota(jnp.int32, (2 * SSD_HEADS, L), 0)
    dt_t = _softplus(dtft + p["row_bias"][...])
    a_t = jnp.where(row < SSD_HEADS, -jnp.exp(p["row_alog"][...]) * dt_t, 0.0)
    acum2_t = _dot(jnp.concatenate(_split3(a_t), axis=1), p["trit3"][...]) * LOG2E
    last2_t = acum2_t[:, L - 1:L]
    w_t = jnp.exp2(last2_t - acum2_t) * dt_t
    arow_t = acum2_t - jnp.log2(dt_t)
    yield

    ri = lax.broadcasted_iota(jnp.int32, (L, L), 0)
    ci = lax.broadcasted_iota(jnp.int32, (L, L), 1)
    causal = ri >= ci

    cb, b_t, c_f = [], [], []
    for g in range(SSD_GROUPS):
        bg = bmat[:, g * SSD_STATE:(g + 1) * SSD_STATE]
        cg = cmat[:, g * SSD_STATE:(g + 1) * SSD_STATE]
        cb.append(_dot_nt(cg.astype(BF16), bg.astype(BF16)))
        b_t.append(bg.T)
        c_f.append(cg)

    g_mats, ce_mats, btw_mats, cdecay = [], [], [], []
    for h in range(SSD_HEADS):
        g = h // (SSD_HEADS // SSD_GROUPS)
        acol = jnp.broadcast_to(acol2[:, h:h + 1], (L, L))
        g_mats.append((cb[g] * jnp.where(causal, jnp.exp2(acol - arow_t[h:h + 1, :]), 0.0)).astype(BF16))
        ce_mats.append((c_f[g] * jnp.exp2(acol)).astype(BF16))
        btw_mats.append((b_t[g] * w_t[h:h + 1, :]).astype(BF16))
        cdecay.append(jnp.exp2(jnp.broadcast_to(last2_t[h:h + 1, :], (L, LANES))))
        yield

    lo_half = lane < SSD_HEAD_DIM
    st_mask = jnp.concatenate([lo_half, jnp.logical_not(lo_half)], axis=0)
    ys = []
    for i in range(N_PAIRS):
        h0, h1 = 2 * i, 2 * i + 1
        xp = xsb[:, i * LANES:(i + 1) * LANES]
        zero = jnp.zeros_like(xp)
        st = st_ref[i]
        rhs = jnp.concatenate([jnp.where(lo_half, xp, zero), jnp.where(lo_half, zero, xp),
                               st.astype(BF16)], axis=0)
        lhs = jnp.concatenate([g_mats[h0], g_mats[h1], ce_mats[h0], ce_mats[h1]], axis=1)
        ys.append(_dot(lhs, rhs))
        snew = _dot(jnp.concatenate([btw_mats[h0], btw_mats[h1]], axis=0), xp)
        cd = jnp.concatenate([cdecay[h0], cdecay[h1]], axis=0)
        st_ref[i] = jnp.where(st_mask, st * cd + snew, 0.0)
        yield

    y = jnp.concatenate(ys, axis=1) + p["d_skip"][...] * xs
    y = y * _silu(z.astype(F32))
    yield
    gw = SSD_WIDTH // SSD_GROUPS
    normed = []
    for g in range(SSD_GROUPS):
        yg = y[:, g * gw:(g + 1) * gw]
        normed.append(yg * lax.rsqrt(jnp.mean(yg * yg, axis=-1, keepdims=True) + EPS))
    y_out((jnp.concatenate(normed, axis=1) * p["norm_w"][...]).astype(BF16))


def _attn_kernel(q_ref, k_ref, vt_ref, cq_ref, ck_ref, o_ref, qx_ref, m_ref, acc_ref):
    t = ATT_TILE
    n_tiles = q_ref.shape[1] // t
    pair = pl.program_id(1)
    cc = ATT_QUERY_CHUNK
    d = ATT_HEAD_DIM

    lane = lax.broadcasted_iota(jnp.int32, (t, LANES), 1)
    lo_half = lane < d
    for qi in range(n_tiles):
        qs = q_ref[0, qi * t:(qi + 1) * t, :]
        cqv = cq_ref[0, qi * t:(qi + 1) * t, :]
        zero = jnp.zeros_like(qs)
        for hh in range(HEADS_PER_LANE_TILE):
            head = pair * HEADS_PER_LANE_TILE + hh
            qm = jnp.where(lo_half, qs, zero) if hh == 0 else jnp.where(lo_half, zero, qs)
            cm = jnp.where((lane // BIAS_LANES_PER_HEAD) == head, cqv, zero)
            qx_ref[qi, hh * t:(hh + 1) * t, :] = jnp.concatenate([qm, cm], axis=1)

    def ones_rows(nk):
        return jnp.where(lax.broadcasted_iota(jnp.int32, (ATT_DENOM_ROWS, nk), 0) == 0, 1.0, 0.0).astype(BF16)

    n_chunks = HEADS_PER_LANE_TILE * t // cc

    def keys_needed(qi, j, c):
        return (c * cc) % t + cc if j == qi else t

    def logits(qi, j, out):
        kx = jnp.concatenate([k_ref[0, j * t:(j + 1) * t, :], ck_ref[0, j * t:(j + 1) * t, :]], axis=1)
        for c in range(n_chunks):
            out[c] = _dot_nt(kx[:keys_needed(qi, j, c)], qx_ref[qi, c * cc:(c + 1) * cc, :])
            yield

    def softmax_pv(qi, j, s_chunks):
        first = j == qi
        for c in range(n_chunks):
            cols = slice(c * cc, (c + 1) * cc)
            hh = (c * cc) // t
            off = (c * cc) % t
            nk = keys_needed(qi, j, c)
            vh = jnp.concatenate([vt_ref[hh * d:(hh + 1) * d, j * t:j * t + nk], ones_rows(nk)], axis=0)
            s = s_chunks[c]
            if first:
                key_i = lax.broadcasted_iota(jnp.int32, (nk, cc), 0)
                qry_i = lax.broadcasted_iota(jnp.int32, (nk, cc), 1)
                s = jnp.where(key_i <= qry_i + off, s, -jnp.inf)
                m_new = jnp.max(s, axis=0, keepdims=True)
                p = jnp.exp2(s - m_new)
                acc_ref[qi, :, cols] = _dot(vh, p.astype(BF16))
            else:
                m_old = m_ref[qi, 0:1, cols]
                m_new = jnp.maximum(m_old, jnp.max(s, axis=0, keepdims=True))
                alpha = jnp.exp2(m_old - m_new)
                p = jnp.exp2(s - m_new)
                acc_ref[qi, :, cols] = alpha * acc_ref[qi, :, cols] + _dot(vh, p.astype(BF16))
            m_ref[qi, 0:1, cols] = m_new
            yield
        if j == (qi - 1 if qi else 0):
            o_t = jnp.concatenate([acc_ref[qi, 0:d, hh * t:(hh + 1) * t] / acc_ref[qi, d:d + 1, hh * t:(hh + 1) * t]
                                   for hh in range(HEADS_PER_LANE_TILE)], axis=0)
            o_ref[0, qi * t:(qi + 1) * t, :] = o_t.T.astype(BF16)

    products = [(qi, qi) for qi in range(n_tiles)] + [(qi, j) for qi in range(n_tiles) for j in range(qi)]
    s_chunks = [dict() for _ in products]
    _trace_round_robin(logits(*products[0], s_chunks[0]), iter(()), ratio=1)
    for n, (qi, j) in enumerate(products):
        nxt = logits(*products[n + 1], s_chunks[n + 1]) if n + 1 < len(products) else iter(())
        _trace_round_robin(nxt, softmax_pv(qi, j, s_chunks[n]), ratio=1)


def _attention(q3, k3, vt, cq3, ck3):
    bsz, seq, _ = q3.shape
    t = ATT_TILE
    n_tiles = seq // t
    cols = HEADS_PER_LANE_TILE * t
    seq_block = lambda lane_block: pl.BlockSpec((1, seq, LANES), lane_block)
    return pl.pallas_call(
        _attn_kernel,
        grid=(bsz, N_PAIRS),
        in_specs=[seq_block(lambda b, p: (b, 0, p)),
                  seq_block(lambda b, p: (b, 0, p)),
                  pl.BlockSpec((LANES, seq), lambda b, p: (p, b)),
                  seq_block(lambda b, p: (b, 0, 0)),
                  seq_block(lambda b, p: (b, 0, 0))],
        out_specs=seq_block(lambda b, p: (b, 0, p)),
        out_shape=jax.ShapeDtypeStruct((bsz, seq, ATT_WIDTH), BF16),
        scratch_shapes=[pltpu.VMEM((n_tiles, cols, 2 * LANES), BF16),
                        pltpu.VMEM((n_tiles, SUBLANES, cols), F32),
                        pltpu.VMEM((n_tiles, ATT_HEAD_DIM + ATT_DENOM_ROWS, cols), F32)],
        compiler_params=pltpu.CompilerParams(dimension_semantics=("arbitrary",) * 2,
                                             vmem_limit_bytes=VMEM_LIMIT_BYTES),
        name="fox_attn",
    )(q3, k3, vt, cq3, ck3)


_SSD_PARAMS = ("conv_w", "conv_b", "col_bias", "col_alog", "row_bias", "row_alog", "d_skip", "norm_w",
               "tri3", "trit3")


def _ssd_mlp_kernel(*refs, tiles_per_batch):
    xbc_ref, z_ref, dtf_ref, dtft_ref, x_ref, ya_ref = refs[:6]
    n_ssd = len(_SSD_PARAMS)
    p = dict(zip(_SSD_PARAMS, refs[6:6 + n_ssd]))
    wos_ref, woa_ref, nmw_ref, wup_ref, wdn_ref, nfw_ref = refs[6 + n_ssd:12 + n_ssd]
    o_ref, tail_ref, st_ref, ys_ref, ys_next_ref = refs[12 + n_ssd:]
    i = pl.program_id(0)

    @pl.when(i % tiles_per_batch == 0)
    def _():
        tail_ref[...] = jnp.zeros_like(tail_ref)
        st_ref[...] = jnp.zeros_like(st_ref)

    @pl.when(i == 0)
    def _():
        ys_ref[...] = jnp.zeros_like(ys_ref)

    nb = MLP_COL_BLOCK

    def mlp_pieces():
        ys, ya = ys_ref[...], ya_ref[...]
        h1_blocks = []
        for j in range(D_MODEL // nb):
            cols = slice(j * nb, (j + 1) * nb)
            h1_blocks.append(x_ref[:, cols] + _dot(ys, wos_ref[:, cols]) + _dot(ya, woa_ref[:, cols]))
            yield
        h1 = jnp.concatenate(h1_blocks, axis=1)
        ms = jnp.mean(h1 * h1, axis=-1, keepdims=True)
        hb = ((h1 * lax.rsqrt(ms + EPS)) * nmw_ref[...]).astype(BF16)
        o_ref[...] = h1
        yield
        for c in range(D_FF // D_MODEL):
            u_blocks = []
            for j in range(D_MODEL // nb):
                cols = slice(c * D_MODEL + j * nb, c * D_MODEL + (j + 1) * nb)
                u = _dot(hb, wup_ref[:, cols])
                u_blocks.append(jnp.square(jnp.maximum(u, 0.0)).astype(BF16))
                yield
            u_all = jnp.concatenate(u_blocks, axis=1)
            for j in range(D_MODEL // nb):
                cols = slice(j * nb, (j + 1) * nb)
                o_ref[:, cols] += _dot(u_all, wdn_ref[c * D_MODEL:(c + 1) * D_MODEL, cols])
                yield
        acc = o_ref[...]
        ms2 = jnp.mean(acc * acc, axis=-1, keepdims=True)
        o_ref[...] = (acc * lax.rsqrt(ms2 + EPS)) * nfw_ref[...]

    def ssd_pieces():
        for c in range(ROW_TILE // CHUNK):
            rows = slice(c * CHUNK, (c + 1) * CHUNK)

            def store(y, rows=rows):
                ys_next_ref[rows, :] = y

            yield from _ssd_chunk(xbc_ref[rows, :], z_ref[rows, :], dtf_ref[rows, :], dtft_ref[:, rows],
                                  p, tail_ref, st_ref, store)

    _trace_round_robin(mlp_pieces(), ssd_pieces(), ratio=SSD_PIECES_PER_MLP_PIECE)
    ys_ref[...] = ys_next_ref[...]


def _trace_round_robin(a, b, ratio):
    done_a = done_b = False
    while not (done_a and done_b):
        if not done_a:
            done_a = next(a, _DONE) is _DONE
        for _ in range(ratio):
            if not done_b:
                done_b = next(b, _DONE) is _DONE


_DONE = object()


def _ssd_mlp(xbc, z, dtf, dtft, x2, ya, ssd_consts, mlp_consts, seq):
    m = x2.shape[0]
    tm = ROW_TILE
    n_tiles = m // tm
    cur = lambda i: jnp.minimum(i, n_tiles - 1)
    prv = lambda i: jnp.maximum(i - 1, 0)
    consts = tuple(ssd_consts) + tuple(mlp_consts)
    return pl.pallas_call(
        functools.partial(_ssd_mlp_kernel, tiles_per_batch=seq // tm),
        grid=(n_tiles + 1,),
        in_specs=[pl.BlockSpec((tm, CONV_CH), lambda i: (cur(i), 0)),
                  pl.BlockSpec((tm, SSD_WIDTH), lambda i: (cur(i), 0)),
                  pl.BlockSpec((tm, LANES), lambda i: (cur(i), 0)),
                  pl.BlockSpec((2 * SSD_HEADS, tm), lambda i: (0, cur(i))),
                  pl.BlockSpec((tm, D_MODEL), lambda i: (prv(i), 0)),
                  pl.BlockSpec((tm, ATT_WIDTH), lambda i: (prv(i), 0))]
                 + [_const_spec(a.shape) for a in consts],
        out_specs=pl.BlockSpec((tm, D_MODEL), lambda i: (prv(i), 0)),
        out_shape=jax.ShapeDtypeStruct((m, D_MODEL), F32),
        scratch_shapes=[pltpu.VMEM((SUBLANES, CONV_CH), F32),
                        pltpu.VMEM((N_PAIRS, 2 * CHUNK, LANES), F32),
                        pltpu.VMEM((tm, SSD_WIDTH), BF16),
                        pltpu.VMEM((tm, SSD_WIDTH), BF16)],
        compiler_params=pltpu.CompilerParams(dimension_semantics=("arbitrary",),
                                             vmem_limit_bytes=VMEM_LIMIT_BYTES),
        name="ssd_mlp",
    )(xbc, z, dtf, dtft, x2, ya, *consts)


def _ssd_constants():
    t = np.arange(CHUNK)
    tri = (t[None, :] <= t[:, None]).astype(np.float32)
    tri3 = np.concatenate([tri, tri, tri], axis=1)
    trit3 = np.concatenate([tri.T, tri.T, tri.T], axis=0)
    pq = np.zeros((3 * LANES, LANES), np.float32)
    pk = np.zeros((3 * LANES, LANES), np.float32)
    oq = np.zeros((1, LANES), np.float32)
    ok = np.zeros((1, LANES), np.float32)
    for h in range(ATT_HEADS):
        base = h * BIAS_LANES_PER_HEAD
        for j in range(3):
            src = j * LANES + SSD_HEADS + h
            pq[src, base + j] = 1.0
            pk[src, base + 3 + j] = -1.0
            ok[0, base + j] = 1.0
            oq[0, base + 3 + j] = 1.0
    return (jnp.asarray(tri3, BF16), jnp.asarray(trit3, BF16), jnp.asarray(pq, BF16), jnp.asarray(pk, BF16),
            jnp.asarray(oq), jnp.asarray(ok))


def kernel(x, norm_mix_w, w_in, conv_w, conv_b, dt_bias, a_log, d_skip, ssd_norm_w, f_bias, w_out,
           norm_mlp_w, w_up, w_down, norm_final_w):
    bsz, seq, _ = x.shape
    m = bsz * seq
    assert norm_mix_w.shape[0] == 1, "single layer"
    assert seq % ROW_TILE == 0 and seq % ATT_TILE == 0 and ROW_TILE % CHUNK == 0
    x2 = x.reshape(m, D_MODEL)

    w = w_in[0]
    o_z, o_xbc = SSD_WIDTH, SSD_WIDTH + CONV_CH
    o_dt = o_xbc + SSD_HEADS
    o_q, o_k, o_v = o_dt + ATT_WIDTH, o_dt + 2 * ATT_WIDTH, o_dt + 3 * ATT_WIDTH
    wz = w[:, :o_z].astype(BF16)
    wxbc = w[:, o_z:o_xbc].astype(BF16)
    wq = w[:, o_dt:o_q].astype(BF16)
    wk = w[:, o_q:o_k].astype(BF16)
    wvt = w[:, o_k:o_v].T.astype(BF16)
    w_dtf = jnp.concatenate([w[:, o_xbc:o_dt], w[:, o_v:]], axis=1)
    wdtf = jnp.pad(w_dtf, ((0, 0), (0, LANES - 2 * SSD_HEADS))).astype(BF16)
    wdtft = w_dtf.T.astype(BF16)

    pad_lanes = lambda a: jnp.pad(a, (0, LANES - a.shape[0]))[None, :]
    col_bias = pad_lanes(jnp.concatenate([dt_bias[0], f_bias[0]]))
    col_alog = pad_lanes(a_log[0])
    tri3, trit3, pq, pk, oq, ok = _ssd_constants()
    z, xbc, q, k, vt, dtf, dtft, cq, ck = _in_proj(x2, norm_mix_w[0][None, :], wz, wxbc, wq, wk, wvt, wdtf, wdtft,
                                                   (col_bias, tri3, pq, pk, oq, ok), seq)

    row_bias = jnp.broadcast_to(jnp.concatenate([dt_bias[0], f_bias[0]])[:, None], (2 * SSD_HEADS, CHUNK))
    row_alog = jnp.broadcast_to(jnp.pad(a_log[0], (0, SSD_HEADS))[:, None], (2 * SSD_HEADS, CHUNK))
    dskip = jnp.repeat(d_skip[0], SSD_HEAD_DIM)[None, :]

    to3 = lambda a: a.reshape(bsz, seq, a.shape[-1])
    y_att = _attention(to3(q), to3(k), vt, to3(cq), to3(ck)).reshape(m, ATT_WIDTH)

    wo = w_out[0].astype(BF16)
    ssd_consts = (conv_w[0], conv_b[0][None, :], col_bias, col_alog, row_bias, row_alog, dskip,
                  ssd_norm_w[0][None, :], tri3, trit3)
    mlp_consts = (wo[:SSD_WIDTH], wo[SSD_WIDTH:], norm_mlp_w[0][None, :], w_up[0].astype(BF16),
                  w_down[0].astype(BF16), norm_final_w[None, :])
    out = _ssd_mlp(xbc, z, dtf, dtft, x2, y_att, ssd_consts, mlp_consts, seq)
    return out.reshape(bsz, seq, D_MODEL)
```
